```python
import math
import jax, jax.numpy as jnp
from jax import lax
import numpy as np

D_MODEL = 1024
BATCH = 8
SEQ = 4096
DEPTH = 1

N_META = 16
DA_HEADS = 8
DA_HEAD_DIM = 64
DA_VDIM = 2 * DA_HEAD_DIM
DA_WIDTH = DA_HEADS * DA_VDIM
RET_HEADS = 4
RET_QK_DIM = D_MODEL // RET_HEADS
RET_V_DIM = 2 * D_MODEL // RET_HEADS
RET_WIDTH = RET_HEADS * RET_V_DIM
RET_CHUNK = 128
Q_BLOCK = 128
D_FF = 4 * D_MODEL
NORM_EPS = 1e-6

SPLITS = (
    DA_HEADS * 2 * DA_HEAD_DIM,
    DA_HEADS * 2 * DA_HEAD_DIM,
    DA_WIDTH,
    RET_HEADS * RET_QK_DIM,
    RET_HEADS * RET_QK_DIM,
    RET_WIDTH,
    RET_WIDTH,
    D_MODEL,
    D_MODEL,
)
D_IN = sum(SPLITS)
SPLIT_POINTS = tuple(int(c) for c in np.cumsum(SPLITS)[:-1])

kernel_name = "hybrid_diffattn_retention_gated_block"


def rmsnorm(x, g):
    xf = x.astype(jnp.float32)
    y = xf * lax.rsqrt(jnp.mean(xf * xf, axis=-1, keepdims=True) + NORM_EPS)
    return (y * g.astype(jnp.float32)).astype(x.dtype)


def head_rms(x):
    xf = x.astype(jnp.float32)
    return (xf * lax.rsqrt(jnp.mean(xf * xf, axis=-1, keepdims=True) + NORM_EPS)).astype(x.dtype)


def alibi_slopes(n):
    return 2.0 ** (-8.0 * (jnp.arange(n, dtype=jnp.float32) + 1.0) / n)


def lambda_init(layer):
    return 0.8 - 0.6 * math.exp(-0.3 * layer)


def diff_attention(q, k, v, lam, lam_init, head_g):
    B, L = q.shape[0], q.shape[1]
    scale = DA_HEAD_DIM ** -0.5
    slopes = alibi_slopes(DA_HEADS)
    pos = jnp.arange(L, dtype=jnp.int32)

    def attend(qb, qpos, kb, vb, kpos):
        s = jnp.einsum('bqhcd,bkhcd->bhcqk', qb, kb).astype(jnp.float32) * scale
        dist = (qpos[:, None] - kpos[None, :]).astype(jnp.float32)
        s = s - slopes[None, :, None, None, None] * dist
        s = jnp.where(dist >= 0, s, -jnp.inf)
        p = jax.nn.softmax(s, axis=-1)
        a = p[:, :, 0] - lam * p[:, :, 1]
        return jnp.einsum('bhqk,bkhd->bqhd', a.astype(vb.dtype), vb)

    o_meta = attend(q[:, :N_META], pos[:N_META], k[:, :N_META], v[:, :N_META], pos[:N_META])
    n_blk = (L - N_META) // Q_BLOCK
    q_real = q[:, N_META:].reshape(B, n_blk, Q_BLOCK, DA_HEADS, 2, DA_HEAD_DIM)
    q_real = jnp.moveaxis(q_real, 1, 0)

    def block_fn(args):
        qb, i = args
        qpos = N_META + i * Q_BLOCK + jnp.arange(Q_BLOCK, dtype=jnp.int32)
        return attend(qb, qpos, k, v, pos)

    o_real = lax.map(block_fn, (q_real, jnp.arange(n_blk, dtype=jnp.int32)))
    o_real = jnp.moveaxis(o_real, 0, 1).reshape(B, L - N_META, DA_HEADS, DA_VDIM)
    o = jnp.concatenate([o_meta, o_real], axis=1)
    o = rmsnorm(o, head_g) * (1.0 - lam_init)
    return o.reshape(B, L, DA_WIDTH)


def retention(q, k, v):
    B, L = q.shape[0], q.shape[1]
    log_gamma = jnp.log(1.0 - 2.0 ** (-5.0 - jnp.arange(RET_HEADS, dtype=jnp.float32)))
    k = k * (RET_QK_DIM ** -0.5)
    q, k, v = (jnp.transpose(t, (0, 2, 1, 3)) for t in (q, k, v))

    def intra_decay(n):
        i = jnp.arange(n, dtype=jnp.float32)
        diff = i[:, None] - i[None, :]
        dec = jnp.exp(jnp.maximum(diff, 0.0)[None] * log_gamma[:, None, None])
        return jnp.where(diff[None] >= 0, dec, 0.0)

    def end_decay(n):
        return jnp.exp((n - 1 - jnp.arange(n, dtype=jnp.float32))[None] * log_gamma[:, None])

    qm, km, vm = q[:, :, :N_META], k[:, :, :N_META], v[:, :, :N_META]
    o_meta = jnp.einsum('bhqk,bhke->bhqe',
                        jnp.einsum('bhqd,bhkd->bhqk', qm, km) * intra_decay(N_META), vm)
    state0 = jnp.einsum('bhkd,bhke->bhde', km * end_decay(N_META)[None, :, :, None], vm)

    n_chunk = (L - N_META) // RET_CHUNK
    def chunks(t):
        t = t[:, :, N_META:].reshape(B, RET_HEADS, n_chunk, RET_CHUNK, t.shape[-1])
        return jnp.moveaxis(t, 2, 0)

    d_c = intra_decay(RET_CHUNK)
    cross_dec = jnp.exp((jnp.arange(RET_CHUNK, dtype=jnp.float32) + 1.0)[None] * log_gamma[:, None])
    end_dec = end_decay(RET_CHUNK)
    chunk_dec = jnp.exp(RET_CHUNK * log_gamma)

    def step(S, inp):
        qc, kc, vc = inp
        inner = jnp.einsum('bhqk,bhke->bhqe', jnp.einsum('bhqd,bhkd->bhqk', qc, kc) * d_c, vc)
        cross = jnp.einsum('bhqd,bhde->bhqe', qc * cross_dec[None, :, :, None], S)
        S_new = S * chunk_dec[None, :, None, None] + jnp.einsum(
            'bhkd,bhke->bhde', kc * end_dec[None, :, :, None], vc)
        return S_new, inner + cross

    _, o_real = lax.scan(step, state0, (chunks(q), chunks(k), chunks(v)))
    o_real = jnp.moveaxis(o_real, 0, 2).reshape(B, RET_HEADS, L - N_META, RET_V_DIM)
    o = jnp.concatenate([o_meta, o_real], axis=2).astype(v.dtype)
    return jnp.transpose(o, (0, 2, 1, 3))


def setup_inputs(seed: int = 0) -> dict:
    key = jax.random.key(seed)
    ks = jax.random.split(key, 20)
    f = jnp.float32
    nrm = lambda k, shape, s: jax.random.normal(k, shape, f) * s
    return {
        "x": nrm(ks[0], (BATCH, SEQ, D_MODEL), 1.0),
        "meta_tokens": nrm(ks[1], (N_META, D_MODEL), 1.0),
        "norm_mix_g": 1.0 + nrm(ks[2], (DEPTH, D_MODEL), 0.02),
        "w_in": nrm(ks[3], (DEPTH, D_MODEL, D_IN), D_MODEL ** -0.5),
        "da_lambda_q1": nrm(ks[4], (DEPTH, DA_HEAD_DIM), 0.1),
        "da_lambda_k1": nrm(ks[5], (DEPTH, DA_HEAD_DIM), 0.1),
        "da_lambda_q2": nrm(ks[6], (DEPTH, DA_HEAD_DIM), 0.1),
        "da_lambda_k2": nrm(ks[7], (DEPTH, DA_HEAD_DIM), 0.1),
        "da_head_g": 1.0 + nrm(ks[8], (DEPTH, DA_VDIM), 0.02),
        "w_proj_da": nrm(ks[9], (DEPTH, DA_WIDTH, D_MODEL), DA_WIDTH ** -0.5),
        "w_proj_ret": nrm(ks[10], (DEPTH, RET_WIDTH, D_MODEL), RET_WIDTH ** -0.5),
        "w_out": nrm(ks[11], (DEPTH, D_MODEL, D_MODEL), D_MODEL ** -0.5),
        "norm_mlp_g": 1.0 + nrm(ks[12], (DEPTH, D_MODEL), 0.02),
        "w_up": nrm(ks[13], (DEPTH, D_MODEL, D_FF), D_MODEL ** -0.5),
        "w_down": nrm(ks[14], (DEPTH, D_FF, D_MODEL), D_FF ** -0.5),
        "final_norm_g": 1.0 + nrm(ks[15], (D_MODEL,), 0.02),
    }


def reference(x, meta_tokens, norm_mix_g, w_in, da_lambda_q1, da_lambda_k1, da_lambda_q2,
              da_lambda_k2, da_head_g, w_proj_da, w_proj_ret, w_out, norm_mlp_g, w_up,
              w_down, final_norm_g):
    B, S, D = x.shape
    L = S + N_META
    meta = jnp.broadcast_to(meta_tokens[None].astype(x.dtype), (B, N_META, D))
    h = jnp.concatenate([meta, x], axis=1)

    for layer in range(DEPTH):
        u = rmsnorm(h, norm_mix_g[layer])
        proj = u @ w_in[layer]
        qa, ka, va, qr, kr, vr, gr_out, g_a, g_r = jnp.split(proj, SPLIT_POINTS, axis=-1)

        lam_init = lambda_init(layer)
        lam = (jnp.exp(jnp.sum(da_lambda_q1[layer].astype(jnp.float32) * da_lambda_k1[layer].astype(jnp.float32)))
               - jnp.exp(jnp.sum(da_lambda_q2[layer].astype(jnp.float32) * da_lambda_k2[layer].astype(jnp.float32)))
               + lam_init)
        y_da = diff_attention(qa.reshape(B, L, DA_HEADS, 2, DA_HEAD_DIM),
                              ka.reshape(B, L, DA_HEADS, 2, DA_HEAD_DIM),
                              va.reshape(B, L, DA_HEADS, DA_VDIM),
                              lam, lam_init, da_head_g[layer])

        o_ret = retention(qr.reshape(B, L, RET_HEADS, RET_QK_DIM),
                          kr.reshape(B, L, RET_HEADS, RET_QK_DIM),
                          vr.reshape(B, L, RET_HEADS, RET_V_DIM))
        y_ret = head_rms(o_ret).reshape(B, L, RET_WIDTH) * jax.nn.silu(gr_out)

        merged = (jax.nn.sigmoid(g_a) * (y_da @ w_proj_da[layer])
                  + jax.nn.sigmoid(g_r) * (y_ret @ w_proj_ret[layer]))
        h = h + merged @ w_out[layer]

        u = rmsnorm(h, norm_mlp_g[layer])
        h = h + jnp.square(jax.nn.relu(u @ w_up[layer])) @ w_down[layer]

    out = rmsnorm(h, final_norm_g)
    return out[:, N_META:]
```

```python
import functools
import math

import numpy as np
import jax
import jax.numpy as jnp
from jax import lax
from jax.experimental import pallas as pl
from jax.experimental.pallas import tpu as pltpu

N_META = 16
DA_HEADS = 8
DA_HEAD_DIM = 64
DA_VDIM = 2 * DA_HEAD_DIM
RET_HEADS = 4
NORM_EPS = 1e-6

F32 = jnp.float32
BF16 = jnp.bfloat16

V7X_VMEM_LIMIT_BYTES = 56 * 1024 * 1024
LANES = 128


def _dot(a, b):
    return jnp.dot(a, b, preferred_element_type=F32)


def _dot_nt(a, b):
    return lax.dot_general(a, b, (((1,), (1,)), ((), ())), preferred_element_type=F32)


def _dot_tn(a, b):
    return lax.dot_general(a, b, (((0,), (0,)), ((), ())), preferred_element_type=F32)


def _rms(xf):
    return xf * lax.rsqrt(jnp.mean(xf * xf, axis=-1, keepdims=True) + NORM_EPS)


def _inproj_kernel(x_ref, g_ref, w_ref, o_ref, u_ref):
    @pl.when(pl.program_id(1) == 0)
    def _():
        u_ref[...] = (_rms(x_ref[...]) * g_ref[...]).astype(BF16)

    o_ref[...] = _dot(u_ref[...], w_ref[...]).astype(o_ref.dtype)


def _inproj(x2, g, w, *, tm, tn):
    n, d = x2.shape
    d_in = w.shape[1]
    return pl.pallas_call(
        _inproj_kernel,
        grid=(n // tm, d_in // tn),
        in_specs=[
            pl.BlockSpec((tm, d), lambda i, j: (i, 0)),
            pl.BlockSpec((1, d), lambda i, j: (0, 0)),
            pl.BlockSpec((d, tn), lambda i, j: (0, j)),
        ],
        out_specs=pl.BlockSpec((tm, tn), lambda i, j: (i, j)),
        out_shape=jax.ShapeDtypeStruct((n, d_in), BF16),
        scratch_shapes=[pltpu.VMEM((tm, d), BF16)],
        compiler_params=pltpu.CompilerParams(
            dimension_semantics=("parallel", "arbitrary"),
            vmem_limit_bytes=V7X_VMEM_LIMIT_BYTES),
    )(x2, g, w)


def _attn_kernel(slopes_ref, q_ref, k_ref, v_ref, mk_ref, mv_ref, lq1_ref, lk1_ref, lq2_ref, lk2_ref,
                 hg_ref, o_ref, m_s, l_s, acc_s, *, tq, lam_init):
    h = pl.program_id(1)
    i = pl.program_id(2)
    slope = slopes_ref[h]
    tk = tq
    neg_inf = jnp.float32(-jnp.inf)

    q = q_ref[...] * jnp.asarray(DA_HEAD_DIM ** -0.5, BF16)
    lane = lax.broadcasted_iota(jnp.int32, q.shape, 1)
    zero = jnp.zeros_like(q)
    q2 = jnp.concatenate([jnp.where(lane < DA_HEAD_DIM, q, zero),
                          jnp.where(lane >= DA_HEAD_DIM, q, zero)], axis=0)

    r = lax.broadcasted_iota(jnp.int32, (2 * tq, tk), 0)
    r = jnp.where(r >= tq, r - tq, r)
    c = lax.broadcasted_iota(jnp.int32, (2 * tq, tk), 1)
    rc = (r - c).astype(F32) * slope

    n_mk = mk_ref.shape[0]
    base_m = (i * tq + N_META).astype(F32) * slope
    z = _dot_nt(q2, mk_ref[...]) - rc[:, :n_mk] - base_m
    z = jnp.where(lax.broadcasted_iota(jnp.int32, z.shape, 1) < N_META, z, neg_inf)
    m0 = jnp.max(z, axis=1, keepdims=True)
    p = jnp.exp(z - m0)
    m_s[...] = m0
    l_s[...] = jnp.sum(p, axis=1, keepdims=True)
    acc_s[...] = _dot(p.astype(BF16), mv_ref[...])

    def block(j, masked):
        start = pl.multiple_of(j * tk, tk)
        s = _dot_nt(q2, k_ref[pl.ds(start, tk), :])
        cj = ((i - j) * tq).astype(F32) * slope
        z = s - rc
        if masked:
            z = jnp.where(c <= r, z, neg_inf)
        m_prev = m_s[...]
        m_new = jnp.maximum(m_prev, jnp.max(z, axis=1, keepdims=True) - cj)
        p = jnp.exp(z - (m_new + cj))
        alpha = jnp.exp(m_prev - m_new)
        l_s[...] = alpha * l_s[...] + jnp.sum(p, axis=1, keepdims=True)
        acc_s[...] = alpha * acc_s[...] + _dot(p.astype(BF16), v_ref[pl.ds(start, tk), :])
        m_s[...] = m_new

    def body(j, carry):
        block(j, masked=False)
        return carry

    lax.fori_loop(0, i, body, 0)
    block(i, masked=True)

    lam = (jnp.exp(jnp.sum(lq1_ref[...] * lk1_ref[...], axis=1, keepdims=True))
           - jnp.exp(jnp.sum(lq2_ref[...] * lk2_ref[...], axis=1, keepdims=True)) + lam_init)
    o12 = acc_s[...] / l_s[...]
    o = o12[:tq] - lam * o12[tq:]
    o_ref[...] = (_rms(o) * hg_ref[...] * (1.0 - lam_init)).astype(o_ref.dtype)


def _diff_attention(proj, mk, mv, lq1, lk1, lq2, lk2, head_g, *, batch, seq, tq, lam_init):
    n = batch * seq
    nq = seq // tq
    dv = DA_VDIM
    k_blk0 = DA_HEADS
    v_blk0 = 2 * DA_HEADS
    slopes = jnp.asarray(2.0 ** (-8.0 * (np.arange(DA_HEADS, dtype=np.float32) + 1.0) / DA_HEADS), F32)
    vec = lambda: pl.BlockSpec((1, DA_HEAD_DIM), lambda b, h, i: (0, 0))
    return pl.pallas_call(
        functools.partial(_attn_kernel, tq=tq, lam_init=lam_init),
        grid=(batch, DA_HEADS, nq),
        in_specs=[
            pl.BlockSpec(memory_space=pltpu.SMEM),
            pl.BlockSpec((tq, dv), lambda b, h, i: (b * nq + i, h)),
            pl.BlockSpec((seq, dv), lambda b, h, i: (b, k_blk0 + h)),
            pl.BlockSpec((seq, dv), lambda b, h, i: (b, v_blk0 + h)),
            pl.BlockSpec((LANES, dv), lambda b, h, i: (0, h)),
            pl.BlockSpec((LANES, dv), lambda b, h, i: (0, h)),
            vec(), vec(), vec(), vec(),
            pl.BlockSpec((1, dv), lambda b, h, i: (0, 0)),
        ],
        out_specs=pl.BlockSpec((tq, dv), lambda b, h, i: (b * nq + i, h)),
        out_shape=jax.ShapeDtypeStruct((n, DA_HEADS * dv), BF16),
        scratch_shapes=[pltpu.VMEM((2 * tq, 1), F32), pltpu.VMEM((2 * tq, 1), F32),
                        pltpu.VMEM((2 * tq, dv), F32)],
        compiler_params=pltpu.CompilerParams(
            dimension_semantics=("parallel", "parallel", "arbitrary"),
            vmem_limit_bytes=V7X_VMEM_LIMIT_BYTES),
    )(slopes, proj, proj, proj, mk, mv, lq1, lk1, lq2, lk2, head_g)


def _ret_kernel(lg_ref, q_ref, k_ref, v_ref, g_ref, mk_ref, mv_ref, o_ref, state_s, dec_s, *, chunk, k_scale):
    h = pl.program_id(1)
    ci = pl.program_id(2)
    lg = lg_ref[h]
    rows = lax.broadcasted_iota(jnp.int32, (chunk, 1), 0).astype(F32)
    end_dec = jnp.exp((chunk - 1.0 - rows) * lg)
    cross_dec = jnp.exp((rows + 1.0) * lg)
    chunk_dec = jnp.exp(jnp.full((1, 1), float(chunk), F32) * lg)
    ks = jnp.asarray(k_scale, BF16)

    @pl.when(ci == 0)
    def _():
        ri = lax.broadcasted_iota(jnp.int32, (chunk, chunk), 0)
        cc = lax.broadcasted_iota(jnp.int32, (chunk, chunk), 1)
        diff = (ri - cc).astype(F32)
        dec_s[...] = jnp.where(diff >= 0, jnp.exp(jnp.maximum(diff, 0.0) * lg), 0.0)
        kd = ((mk_ref[...] * ks).astype(F32) * end_dec).astype(BF16)
        state_s[...] = _dot_tn(kd, mv_ref[...])

    qc = q_ref[...]
    kc = k_ref[...] * ks
    vc = v_ref[...]
    a = _dot_nt(qc, kc) * dec_s[...]
    state = state_s[...]
    o = _dot(a.astype(BF16), vc) + _dot(qc, state.astype(BF16)) * cross_dec
    kd = (kc.astype(F32) * end_dec).astype(BF16)
    state_s[...] = state * chunk_dec + _dot_tn(kd, vc)
    g = g_ref[...].astype(F32)
    o_ref[...] = (_rms(o) * (g * jax.nn.sigmoid(g))).astype(o_ref.dtype)


def _retention(proj, mk, mv, *, batch, seq, chunk, d_model):
    n = batch * seq
    nc = seq // chunk
    dk = d_model // RET_HEADS
    dv = 2 * d_model // RET_HEADS
    q_blk0 = (3 * d_model) // dk
    k_blk0 = (4 * d_model) // dk
    v_blk0 = (5 * d_model) // dv
    g_blk0 = (7 * d_model) // dv
    log_gamma = jnp.asarray(np.log(1.0 - 2.0 ** (-5.0 - np.arange(RET_HEADS, dtype=np.float32))), F32)
    return pl.pallas_call(
        functools.partial(_ret_kernel, chunk=chunk, k_scale=dk ** -0.5),
        grid=(batch, RET_HEADS, nc),
        in_specs=[
            pl.BlockSpec(memory_space=pltpu.SMEM),
            pl.BlockSpec((chunk, dk), lambda b, h, c: (b * nc + c, q_blk0 + h)),
            pl.BlockSpec((chunk, dk), lambda b, h, c: (b * nc + c, k_blk0 + h)),
            pl.BlockSpec((chunk, dv), lambda b, h, c: (b * nc + c, v_blk0 + h)),
            pl.BlockSpec((chunk, dv), lambda b, h, c: (b * nc + c, g_blk0 + h)),
            pl.BlockSpec((chunk, dk), lambda b, h, c: (0, h)),
            pl.BlockSpec((chunk, dv), lambda b, h, c: (0, h)),
        ],
        out_specs=pl.BlockSpec((chunk, dv), lambda b, h, c: (b * nc + c, h)),
        out_shape=jax.ShapeDtypeStruct((n, RET_HEADS * dv), BF16),
        scratch_shapes=[pltpu.VMEM((dk, dv), F32), pltpu.VMEM((chunk, chunk), F32)],
        compiler_params=pltpu.CompilerParams(
            dimension_semantics=("parallel", "parallel", "arbitrary"),
            vmem_limit_bytes=V7X_VMEM_LIMIT_BYTES),
    )(log_gamma, proj, proj, proj, proj, mk, mv)


def _merge_kernel(x_ref, yda_ref, yret_ref, ga_ref, gr_ref, wda_ref, wret_ref, wout_ref, o_ref):
    a = _dot(yda_ref[...], wda_ref[...])
    r = _dot(yret_ref[...], wret_ref[...])
    merged = (jax.nn.sigmoid(ga_ref[...].astype(F32)) * a
              + jax.nn.sigmoid(gr_ref[...].astype(F32)) * r)
    o_ref[...] = x_ref[...] + _dot(merged.astype(BF16), wout_ref[...])


def _merge(x2, y_da, y_ret, proj, w_da, w_ret, w_out, *, tm):
    n, d = x2.shape
    ga_blk = 9
    gr_blk = 10
    const = lambda shape: pl.BlockSpec(shape, lambda i: (0, 0), pipeline_mode=pl.Buffered(1))
    return pl.pallas_call(
        _merge_kernel,
        grid=(n // tm,),
        in_specs=[
            pl.BlockSpec((tm, d), lambda i: (i, 0)),
            pl.BlockSpec((tm, y_da.shape[1]), lambda i: (i, 0)),
            pl.BlockSpec((tm, y_ret.shape[1]), lambda i: (i, 0)),
            pl.BlockSpec((tm, d), lambda i: (i, ga_blk)),
            pl.BlockSpec((tm, d), lambda i: (i, gr_blk)),
            const(w_da.shape), const(w_ret.shape), const(w_out.shape),
        ],
        out_specs=pl.BlockSpec((tm, d), lambda i: (i, 0)),
        out_shape=jax.ShapeDtypeStruct((n, d), F32),
        compiler_params=pltpu.CompilerParams(
            dimension_semantics=("parallel",),
            vmem_limit_bytes=V7X_VMEM_LIMIT_BYTES),
    )(x2, y_da, y_ret, proj, proj, w_da, w_ret, w_out)


def _mlp_kernel(h_ref, g1_ref, wup_ref, wdown_ref, g2_ref, o_ref, *, tf):
    hres = h_ref[...]
    u = (_rms(hres) * g1_ref[...]).astype(BF16)
    acc = hres
    for f in range(0, wup_ref.shape[1], tf):
        a = jnp.maximum(_dot(u, wup_ref[:, f:f + tf]), 0.0)
        acc = acc + _dot((a * a).astype(BF16), wdown_ref[f:f + tf, :])
    o_ref[...] = _rms(acc) * g2_ref[...]


def _mlp(h1, g1, w_up, w_down, g2, *, tm, tf):
    n, d = h1.shape
    const = lambda shape: pl.BlockSpec(shape, lambda i: (0, 0), pipeline_mode=pl.Buffered(1))
    return pl.pallas_call(
        functools.partial(_mlp_kernel, tf=tf),
        grid=(n // tm,),
        in_specs=[
            pl.BlockSpec((tm, d), lambda i: (i, 0)),
            pl.BlockSpec((1, d), lambda i: (0, 0)),
            const(w_up.shape), const(w_down.shape),
            pl.BlockSpec((1, d), lambda i: (0, 0)),
        ],
        out_specs=pl.BlockSpec((tm, d), lambda i: (i, 0)),
        out_shape=jax.ShapeDtypeStruct((n, d), F32),
        compiler_params=pltpu.CompilerParams(
            dimension_semantics=("parallel",),
            vmem_limit_bytes=V7X_VMEM_LIMIT_BYTES),
    )(h1, g1, w_up, w_down, g2)


def _tile(n, target):
    t = min(n, target)
    assert n % t == 0, (n, t)
    return t


def kernel(x, meta_tokens, norm_mix_g, w_in, da_lambda_q1, da_lambda_k1, da_lambda_q2, da_lambda_k2,
           da_head_g, w_proj_da, w_proj_ret, w_out, norm_mlp_g, w_up, w_down, final_norm_g):
    batch, seq, d = x.shape
    assert w_in.shape[0] == 1, "single-layer block"
    assert d // RET_HEADS == 2 * LANES and DA_HEADS * DA_VDIM == d
    n = batch * seq
    layer = 0
    lam_init = 0.8 - 0.6 * math.exp(-0.3 * layer)

    x2 = x.reshape(n, d)
    w_in_b = w_in[layer].astype(BF16)
    g_mix = norm_mix_g[layer].reshape(1, d)

    proj = _inproj(x2, g_mix, w_in_b, tm=_tile(n, 1024), tn=_tile(w_in_b.shape[1], 1024))
    proj_m = _inproj(meta_tokens.astype(x.dtype), g_mix, w_in_b, tm=N_META, tn=_tile(w_in_b.shape[1], 1024))

    pad_a = ((0, LANES - N_META), (0, 0))
    mk_a = jnp.pad(proj_m[:, d:2 * d], pad_a)
    mv_a = jnp.pad(proj_m[:, 2 * d:3 * d], pad_a)
    row = lambda v: v[layer].reshape(1, -1).astype(F32)
    y_da = _diff_attention(proj, mk_a, mv_a, row(da_lambda_q1), row(da_lambda_k1), row(da_lambda_q2),
                           row(da_lambda_k2), row(da_head_g), batch=batch, seq=seq,
                           tq=_tile(seq, 256), lam_init=lam_init)

    chunk = _tile(seq, 256)
    pad_r = ((chunk - N_META, 0), (0, 0))
    mk_r = jnp.pad(proj_m[:, 4 * d:5 * d], pad_r)
    mv_r = jnp.pad(proj_m[:, 5 * d:7 * d], pad_r)
    y_ret = _retention(proj, mk_r, mv_r, batch=batch, seq=seq, chunk=chunk, d_model=d)

    h1 = _merge(x2, y_da, y_ret, proj, w_proj_da[layer].astype(BF16), w_proj_ret[layer].astype(BF16),
                w_out[layer].astype(BF16), tm=_tile(n, 256))
    out = _mlp(h1, norm_mlp_g[layer].reshape(1, d), w_up[layer].astype(BF16), w_down[layer].astype(BF16),
               final_norm_g.reshape(1, d), tm=_tile(n, 256), tf=1024)
    return out.reshape(batch, seq, d)
```

```python
import functools
import math

import numpy as np
import jax
import jax.numpy as jnp
from jax import lax
from jax.experimental import pallas as pl
from jax.experimental.pallas import tpu as pltpu

N_META = 16
DA_HEADS = 8
DA_HEAD_DIM = 64
DA_VDIM = 2 * DA_HEAD_DIM
RET_HEADS = 4
NORM_EPS = 1e-6

F32 = jnp.float32
BF16 = jnp.bfloat16

V7X_VMEM_LIMIT_BYTES = 56 * 1024 * 1024
LANES = 128
BF16_SUBLANES = 16


def _dot(a, b):
    return jnp.dot(a, b, preferred_element_type=F32)


def _dot_nt(a, b):
    return lax.dot_general(a, b, (((1,), (1,)), ((), ())), preferred_element_type=F32)


def _dot_tn(a, b):
    return lax.dot_general(a, b, (((0,), (0,)), ((), ())), preferred_element_type=F32)


def _rms(xf):
    return xf * lax.rsqrt(jnp.mean(xf * xf, axis=-1, keepdims=True) + NORM_EPS)


def _inproj_kernel(x_ref, g_ref, w_ref, o_ref, u_ref):
    @pl.when(pl.program_id(1) == 0)
    def _():
        u_ref[...] = (_rms(x_ref[...]) * g_ref[...]).astype(BF16)

    o_ref[...] = _dot(u_ref[...], w_ref[...]).astype(o_ref.dtype)


def _inproj(x2, g, w, *, tm, tn):
    n, d = x2.shape
    d_in = w.shape[1]
    return pl.pallas_call(
        _inproj_kernel,
        grid=(n // tm, d_in // tn),
        in_specs=[
            pl.BlockSpec((tm, d), lambda i, j: (i, 0)),
            pl.BlockSpec((1, d), lambda i, j: (0, 0)),
            pl.BlockSpec((d, tn), lambda i, j: (0, j)),
        ],
        out_specs=pl.BlockSpec((tm, tn), lambda i, j: (i, j)),
        out_shape=jax.ShapeDtypeStruct((n, d_in), BF16),
        scratch_shapes=[pltpu.VMEM((tm, d), BF16)],
        compiler_params=pltpu.CompilerParams(
            dimension_semantics=("parallel", "arbitrary"),
            vmem_limit_bytes=V7X_VMEM_LIMIT_BYTES),
    )(x2, g, w)


def _attn_kernel(slopes_ref, q_ref, k_ref, v_ref, mk_ref, mv_ref, lq1_ref, lk1_ref, lq2_ref, lk2_ref,
                 hg_ref, o_ref, mvt_s, vt_s, m_s, acc_s, *, tq, lam_init):
    h = pl.program_id(1)
    i = pl.program_id(2)
    slope = slopes_ref[h]
    tk = tq
    dv = v_ref.shape[1]
    n_mk = mk_ref.shape[0]
    neg_inf = jnp.float32(-jnp.inf)

    @pl.when(i == 0)
    def _():
        n_ones = mvt_s.shape[0] - dv
        mvt_s[dv:, :] = jnp.ones((n_ones, n_mk), mvt_s.dtype)
        mvt_s[:dv, :] = mv_ref[...].astype(F32).T.astype(mvt_s.dtype)
        for c in range(vt_s.shape[0]):
            vt_s[c, dv:, :] = jnp.ones((n_ones, tk), vt_s.dtype)
            vt_s[c, :dv, :] = v_ref[c * tk:(c + 1) * tk, :].astype(F32).T.astype(vt_s.dtype)

    q = q_ref[...].astype(F32) * (DA_HEAD_DIM ** -0.5)
    lane = lax.broadcasted_iota(jnp.int32, q.shape, 1)
    feat = jnp.where(lane < 2, 1.0, 0.0).astype(F32)
    qe = jnp.concatenate(
        [jnp.concatenate([jnp.where(lane < DA_HEAD_DIM, q, 0.0), feat], axis=1),
         jnp.concatenate([jnp.where(lane >= DA_HEAD_DIM, q, 0.0), feat], axis=1)], axis=0)
    qt = qe.T.astype(BF16)

    krow = lax.broadcasted_iota(jnp.int32, (tk, LANES), 0).astype(F32) * slope
    klane = lax.broadcasted_iota(jnp.int32, (tk, LANES), 1)
    k_hi = krow.astype(BF16).astype(F32)
    kfeat = jnp.where(klane == 0, k_hi, jnp.where(klane == 1, krow - k_hi, 0.0)).astype(BF16)

    def scores(k_blk, n_rows):
        return _dot(jnp.concatenate([k_blk, kfeat[:n_rows]], axis=1), qt)

    s = scores(mk_ref[...], n_mk)
    s = jnp.where(lax.broadcasted_iota(jnp.int32, s.shape, 0) < N_META, s, neg_inf)
    m0 = jnp.max(s, axis=0, keepdims=True)
    p = jnp.exp(s - m0)
    m_s[...] = m0 - (i * tq + N_META).astype(F32) * slope
    acc_s[...] = _dot(mvt_s[...], p.astype(BF16))

    def block(j, masked):
        start = pl.multiple_of(j * tk, tk)
        s = scores(k_ref[pl.ds(start, tk), :], tk)
        if masked:
            kr = lax.broadcasted_iota(jnp.int32, s.shape, 0)
            qc = lax.broadcasted_iota(jnp.int32, s.shape, 1)
            qc = jnp.where(qc >= tq, qc - tq, qc)
            s = jnp.where(kr <= qc, s, neg_inf)
        cj = ((j - i) * tq).astype(F32) * slope
        m_prev = m_s[...]
        m_new = jnp.maximum(m_prev, jnp.max(s, axis=0, keepdims=True) + cj)
        p = jnp.exp(s - (m_new - cj))
        alpha = jnp.exp(m_prev - m_new)
        acc_s[...] = alpha * acc_s[...] + _dot(vt_s[j], p.astype(BF16))
        m_s[...] = m_new

    def body(j, carry):
        block(j, masked=False)
        return carry

    lax.fori_loop(0, i, body, 0)
    block(i, masked=True)

    lam = (jnp.exp(jnp.sum(lq1_ref[...] * lk1_ref[...], axis=1, keepdims=True))
           - jnp.exp(jnp.sum(lq2_ref[...] * lk2_ref[...], axis=1, keepdims=True)) + lam_init)
    acc = acc_s[...]
    o12 = acc[:dv] / acc[dv:dv + 1]
    ot = o12[:, :tq] - lam * o12[:, tq:]
    yt = ot * lax.rsqrt(jnp.mean(ot * ot, axis=0, keepdims=True) + NORM_EPS)
    o_ref[...] = (yt.T * hg_ref[...] * (1.0 - lam_init)).astype(o_ref.dtype)


def _diff_attention(proj, mk, mv, lq1, lk1, lq2, lk2, head_g, *, batch, seq, tq, lam_init):
    n = batch * seq
    nq = seq // tq
    dv = DA_VDIM
    n_mk = mk.shape[0]
    k_blk0 = DA_HEADS
    v_blk0 = 2 * DA_HEADS
    slopes = jnp.asarray(2.0 ** (-8.0 * (np.arange(DA_HEADS, dtype=np.float32) + 1.0) / DA_HEADS), F32)
    vec = lambda: pl.BlockSpec((1, DA_HEAD_DIM), lambda b, h, i: (0, 0))
    return pl.pallas_call(
        functools.partial(_attn_kernel, tq=tq, lam_init=lam_init),
        grid=(batch, DA_HEADS, nq),
        in_specs=[
            pl.BlockSpec(memory_space=pltpu.SMEM),
            pl.BlockSpec((tq, dv), lambda b, h, i: (b * nq + i, h)),
            pl.BlockSpec((seq, dv), lambda b, h, i: (b, k_blk0 + h)),
            pl.BlockSpec((seq, dv), lambda b, h, i: (b, v_blk0 + h)),
            pl.BlockSpec((n_mk, dv), lambda b, h, i: (0, h)),
            pl.BlockSpec((n_mk, dv), lambda b, h, i: (0, h)),
            vec(), vec(), vec(), vec(),
            pl.BlockSpec((1, dv), lambda b, h, i: (0, 0)),
        ],
        out_specs=pl.BlockSpec((tq, dv), lambda b, h, i: (b * nq + i, h)),
        out_shape=jax.ShapeDtypeStruct((n, DA_HEADS * dv), BF16),
        scratch_shapes=[pltpu.VMEM((dv + BF16_SUBLANES, n_mk), BF16),
                        pltpu.VMEM((nq, dv + BF16_SUBLANES, tq), BF16),
                        pltpu.VMEM((1, 2 * tq), F32),
                        pltpu.VMEM((dv + BF16_SUBLANES, 2 * tq), F32)],
        compiler_params=pltpu.CompilerParams(
            dimension_semantics=("parallel", "parallel", "arbitrary"),
            vmem_limit_bytes=V7X_VMEM_LIMIT_BYTES),
    )(slopes, proj, proj, proj, mk, mv, lq1, lk1, lq2, lk2, head_g)


def _ret_kernel(lg_ref, q_ref, k_ref, v_ref, g_ref, mk_ref, mv_ref, o_ref, state_s, dec_s, *, chunk, k_scale):
    h = pl.program_id(1)
    ci = pl.program_id(2)
    lg = lg_ref[h]
    rows = lax.broadcasted_iota(jnp.int32, (chunk, 1), 0).astype(F32)
    end_dec = jnp.exp((chunk - 1.0 - rows) * lg)
    cross_dec = jnp.exp((rows + 1.0) * lg)
    chunk_dec = jnp.exp(jnp.full((1, 1), float(chunk), F32) * lg)
    ks = jnp.asarray(k_scale, BF16)

    @pl.when(ci == 0)
    def _():
        ri = lax.broadcasted_iota(jnp.int32, (chunk, chunk), 0)
        cc = lax.broadcasted_iota(jnp.int32, (chunk, chunk), 1)
        diff = (ri - cc).astype(F32)
        dec_s[...] = jnp.where(diff >= 0, jnp.exp(jnp.maximum(diff, 0.0) * lg), 0.0)
        kd = ((mk_ref[...] * ks).astype(F32) * end_dec).astype(BF16)
        state_s[...] = _dot_tn(kd, mv_ref[...])

    qc = q_ref[...]
    kc = k_ref[...] * ks
    vc = v_ref[...]
    a = _dot_nt(qc, kc) * dec_s[...]
    state = state_s[...]
    o = _dot(a.astype(BF16), vc) + _dot(qc, state.astype(BF16)) * cross_dec
    kd = (kc.astype(F32) * end_dec).astype(BF16)
    state_s[...] = state * chunk_dec + _dot_tn(kd, vc)
    g = g_ref[...].astype(F32)
    o_ref[...] = (_rms(o) * (g * jax.nn.sigmoid(g))).astype(o_ref.dtype)


def _retention(proj, mk, mv, *, batch, seq, chunk, d_model):
    n = batch * seq
    nc = seq // chunk
    dk = d_model // RET_HEADS
    dv = 2 * d_model // RET_HEADS
    q_blk0 = (3 * d_model) // dk
    k_blk0 = (4 * d_model) // dk
    v_blk0 = (5 * d_model) // dv
    g_blk0 = (7 * d_model) // dv
    log_gamma = jnp.asarray(np.log(1.0 - 2.0 ** (-5.0 - np.arange(RET_HEADS, dtype=np.float32))), F32)
    return pl.pallas_call(
        functools.partial(_ret_kernel, chunk=chunk, k_scale=dk ** -0.5),
        grid=(batch, RET_HEADS, nc),
        in_specs=[
            pl.BlockSpec(memory_space=pltpu.SMEM),
            pl.BlockSpec((chunk, dk), lambda b, h, c: (b * nc + c, q_blk0 + h)),
            pl.BlockSpec((chunk, dk), lambda b, h, c: (b * nc + c, k_blk0 + h)),
            pl.BlockSpec((chunk, dv), lambda b, h, c: (b * nc + c, v_blk0 + h)),
            pl.BlockSpec((chunk, dv), lambda b, h, c: (b * nc + c, g_blk0 + h)),
            pl.BlockSpec((chunk, dk), lambda b, h, c: (0, h)),
            pl.BlockSpec((chunk, dv), lambda b, h, c: (0, h)),
        ],
        out_specs=pl.BlockSpec((chunk, dv), lambda b, h, c: (b * nc + c, h)),
        out_shape=jax.ShapeDtypeStruct((n, RET_HEADS * dv), BF16),
        scratch_shapes=[pltpu.VMEM((dk, dv), F32), pltpu.VMEM((chunk, chunk), F32)],
        compiler_params=pltpu.CompilerParams(
            dimension_semantics=("parallel", "parallel", "arbitrary"),
            vmem_limit_bytes=V7X_VMEM_LIMIT_BYTES),
    )(log_gamma, proj, proj, proj, proj, mk, mv)


def _merge_kernel(x_ref, yda_ref, yret_ref, ga_ref, gr_ref, wda_ref, wret_ref, wout_ref, o_ref):
    a = _dot(yda_ref[...], wda_ref[...])
    r = _dot(yret_ref[...], wret_ref[...])
    merged = (jax.nn.sigmoid(ga_ref[...].astype(F32)) * a
              + jax.nn.sigmoid(gr_ref[...].astype(F32)) * r)
    o_ref[...] = x_ref[...] + _dot(merged.astype(BF16), wout_ref[...])


def _merge(x2, y_da, y_ret, proj, w_da, w_ret, w_out, *, tm):
    n, d = x2.shape
    ga_blk = 9
    gr_blk = 10
    const = lambda shape: pl.BlockSpec(shape, lambda i: (0, 0), pipeline_mode=pl.Buffered(1))
    return pl.pallas_call(
        _merge_kernel,
        grid=(n // tm,),
        in_specs=[
            pl.BlockSpec((tm, d), lambda i: (i, 0)),
            pl.BlockSpec((tm, y_da.shape[1]), lambda i: (i, 0)),
            pl.BlockSpec((tm, y_ret.shape[1]), lambda i: (i, 0)),
            pl.BlockSpec((tm, d), lambda i: (i, ga_blk)),
            pl.BlockSpec((tm, d), lambda i: (i, gr_blk)),
            const(w_da.shape), const(w_ret.shape), const(w_out.shape),
        ],
        out_specs=pl.BlockSpec((tm, d), lambda i: (i, 0)),
        out_shape=jax.ShapeDtypeStruct((n, d), F32),
        compiler_params=pltpu.CompilerParams(
            dimension_semantics=("parallel",),
            vmem_limit_bytes=V7X_VMEM_LIMIT_BYTES),
    )(x2, y_da, y_ret, proj, proj, w_da, w_ret, w_out)


def _mlp_kernel(h_ref, g1_ref, wup_ref, wdown_ref, g2_ref, o_ref, *, tf):
    hres = h_ref[...]
    u = (_rms(hres) * g1_ref[...]).astype(BF16)
    acc = hres
    for f in range(0, wup_ref.shape[1], tf):
        a = jnp.maximum(_dot(u, wup_ref[:, f:f + tf]), 0.0)
        acc = acc + _dot((a * a).astype(BF16), wdown_ref[f:f + tf, :])
    o_ref[...] = _rms(acc) * g2_ref[...]


def _mlp(h1, g1, w_up, w_down, g2, *, tm, tf):
    n, d = h1.shape
    const = lambda shape: pl.BlockSpec(shape, lambda i: (0, 0), pipeline_mode=pl.Buffered(1))
    return pl.pallas_call(
        functools.partial(_mlp_kernel, tf=tf),
        grid=(n // tm,),
        in_specs=[
            pl.BlockSpec((tm, d), lambda i: (i, 0)),
            pl.BlockSpec((1, d), lambda i: (0, 0)),
            const(w_up.shape), const(w_down.shape),
            pl.BlockSpec((1, d), lambda i: (0, 0)),
        ],
        out_specs=pl.BlockSpec((tm, d), lambda i: (i, 0)),
        out_shape=jax.ShapeDtypeStruct((n, d), F32),
        compiler_params=pltpu.CompilerParams(
            dimension_semantics=("parallel",),
            vmem_limit_bytes=V7X_VMEM_LIMIT_BYTES),
    )(h1, g1, w_up, w_down, g2)


def _tile(n, target):
    t = min(n, target)
    assert n % t == 0, (n, t)
    return t


def kernel(x, meta_tokens, norm_mix_g, w_in, da_lambda_q1, da_lambda_k1, da_lambda_q2, da_lambda_k2,
           da_head_g, w_proj_da, w_proj_ret, w_out, norm_mlp_g, w_up, w_down, final_norm_g):
    batch, seq, d = x.shape
    assert w_in.shape[0] == 1, "single-layer block"
    assert d // RET_HEADS == 2 * LANES and DA_HEADS * DA_VDIM == d
    n = batch * seq
    layer = 0
    lam_init = 0.8 - 0.6 * math.exp(-0.3 * layer)

    x2 = x.reshape(n, d)
    w_in_b = w_in[layer].astype(BF16)
    g_mix = norm_mix_g[layer].reshape(1, d)

    proj = _inproj(x2, g_mix, w_in_b, tm=_tile(n, 1024), tn=_tile(w_in_b.shape[1], 1024))
    proj_m = _inproj(meta_tokens.astype(x.dtype), g_mix, w_in_b, tm=N_META, tn=_tile(w_in_b.shape[1], 1024))

    pad_a = ((0, LANES - N_META), (0, 0))
    mk_a = jnp.pad(proj_m[:, d:2 * d], pad_a)
    mv_a = jnp.pad(proj_m[:, 2 * d:3 * d], pad_a)
    row = lambda v: v[layer].reshape(1, -1).astype(F32)
    y_da = _diff_attention(proj, mk_a, mv_a, row(da_lambda_q1), row(da_lambda_k1), row(da_lambda_q2),
                           row(da_lambda_k2), row(da_head_g), batch=batch, seq=seq,
                           tq=_tile(seq, 256), lam_init=lam_init)

    chunk = _tile(seq, 256)
    pad_r = ((chunk - N_META, 0), (0, 0))
    mk_r = jnp.pad(proj_m[:, 4 * d:5 * d], pad_r)
    mv_r = jnp.pad(proj_m[:, 5 * d:7 * d], pad_r)
    y_ret = _retention(proj, mk_r, mv_r, batch=batch, seq=seq, chunk=chunk, d_model=d)

    h1 = _merge(x2, y_da, y_ret, proj, w_proj_da[layer].astype(BF16), w_proj_ret[layer].astype(BF16),
                w_out[layer].astype(BF16), tm=_tile(n, 256))
    out = _mlp(h1, norm_mlp_g[layer].reshape(1, d), w_up[layer].astype(BF16), w_down[layer].astype(BF16),
               final_norm_g.reshape(1, d), tm=_tile(n, 256), tf=1024)
    return out.reshape(batch, seq, d)
```

```python
import functools
import math

import numpy as np
import jax
import jax.numpy as jnp
from jax import lax
from jax.experimental import pallas as pl
from jax.experimental.pallas import tpu as pltpu

N_META = 16
DA_HEADS = 8
DA_HEAD_DIM = 64
DA_VDIM = 2 * DA_HEAD_DIM
RET_HEADS = 4
NORM_EPS = 1e-6

F32 = jnp.float32
BF16 = jnp.bfloat16

V7X_VMEM_LIMIT_BYTES = 56 * 1024 * 1024
LANES = 128
BF16_SUBLANES = 16


def _dot(a, b):
    return jnp.dot(a, b, preferred_element_type=F32)


def _dot_nt(a, b):
    return lax.dot_general(a, b, (((1,), (1,)), ((), ())), preferred_element_type=F32)


def _dot_tn(a, b):
    return lax.dot_general(a, b, (((0,), (0,)), ((), ())), preferred_element_type=F32)


def _rms(xf):
    return xf * lax.rsqrt(jnp.mean(xf * xf, axis=-1, keepdims=True) + NORM_EPS)


def _inproj_kernel(x_ref, g_ref, w_ref, o_ref, u_ref):
    @pl.when(pl.program_id(1) == 0)
    def _():
        u_ref[...] = (_rms(x_ref[...]) * g_ref[...]).astype(BF16)

    o_ref[...] = _dot(u_ref[...], w_ref[...]).astype(o_ref.dtype)


def _inproj(x2, g, w, *, tm, tn):
    n, d = x2.shape
    d_in = w.shape[1]
    return pl.pallas_call(
        _inproj_kernel,
        grid=(n // tm, d_in // tn),
        in_specs=[
            pl.BlockSpec((tm, d), lambda i, j: (i, 0)),
            pl.BlockSpec((1, d), lambda i, j: (0, 0)),
            pl.BlockSpec((d, tn), lambda i, j: (0, j)),
        ],
        out_specs=pl.BlockSpec((tm, tn), lambda i, j: (i, j)),
        out_shape=jax.ShapeDtypeStruct((n, d_in), BF16),
        scratch_shapes=[pltpu.VMEM((tm, d), BF16)],
        compiler_params=pltpu.CompilerParams(
            dimension_semantics=("parallel", "arbitrary"),
            vmem_limit_bytes=V7X_VMEM_LIMIT_BYTES),
    )(x2, g, w)


ATTN_TQ = 512
ATTN_SUB = 256
ATTN_CHUNK = 1024
N_FEAT = 4


def _attn_kernel(slopes_ref, q_ref, k_ref, v_ref, mk_ref, mv_ref, lq1_ref, lk1_ref, lq2_ref, lk2_ref,
                 hg_ref, o_ref, mvt_s, vt_s, kpos_s, qt_s, s_s, sm_s, m_s, acc_s, *, lam_init):
    h = pl.program_id(1)
    i = pl.program_id(2)
    slope = slopes_ref[h]
    tq, sub, chunk = ATTN_TQ, ATTN_SUB, ATTN_CHUNK
    dv = v_ref.shape[1]
    n_mk = mk_ref.shape[0]
    neg_inf = jnp.float32(-jnp.inf)
    lane_row = lax.broadcasted_iota(jnp.int32, (1, LANES), 1)

    @pl.when(i == 0)
    def _():
        n_ones = mvt_s.shape[0] - dv
        mvt_s[dv:, :] = jnp.ones((n_ones, n_mk), mvt_s.dtype)
        mvt_s[:dv, :] = mv_ref[...].astype(F32).T.astype(mvt_s.dtype)
        for c in range(vt_s.shape[0]):
            vt_s[c, dv:, :] = jnp.ones((n_ones, tq), vt_s.dtype)
            for c0 in range(0, tq, sub):
                vt_s[c, :dv, c0:c0 + sub] = (
                    v_ref[c * tq + c0:c * tq + c0 + sub, :].astype(F32).T.astype(vt_s.dtype))
        pos = lax.broadcasted_iota(jnp.int32, (chunk, LANES), 0).astype(F32) * slope
        plane = lax.broadcasted_iota(jnp.int32, (chunk, LANES), 1)
        pos_hi = pos.astype(BF16).astype(F32)
        kpos_s[...] = jnp.where(plane == 0, pos_hi, jnp.where(plane == 1, pos - pos_hi, 0.0))

    q = q_ref[...].astype(F32) * (DA_HEAD_DIM ** -0.5)
    lane = lax.broadcasted_iota(jnp.int32, (sub, LANES), 1)
    feat = jnp.where(lane < N_FEAT, 1.0, 0.0).astype(F32)
    groups = []
    for half in (q[:sub], q[sub:]):
        groups.append(jnp.concatenate([jnp.where(lane < DA_HEAD_DIM, half, 0.0), feat], axis=1))
        groups.append(jnp.concatenate([jnp.where(lane >= DA_HEAD_DIM, half, 0.0), feat], axis=1))
    qt_s[...] = jnp.concatenate(groups, axis=0).T.astype(BF16)

    def key_feats(n_rows, offset):
        off = jnp.full((1, LANES), offset, F32)
        off_hi = off.astype(BF16).astype(F32)
        off_row = jnp.where(lane_row == 2, off_hi, jnp.where(lane_row == 3, off - off_hi, 0.0))
        return (kpos_s[:n_rows, :] + off_row).astype(BF16)

    def scores(k_blk, n_rows, offset):
        return _dot(jnp.concatenate([k_blk, key_feats(n_rows, offset)], axis=1), qt_s[...])

    s = scores(mk_ref[...], n_mk, -(i * tq + N_META).astype(F32) * slope)
    s = jnp.where(lax.broadcasted_iota(jnp.int32, s.shape, 0) < N_META, s, neg_inf)
    sm_s[...] = s
    m_s[...] = jnp.max(s, axis=0, keepdims=True)

    def score_chunk(start, n_rows, masked):
        s = scores(k_ref[pl.ds(start, n_rows), :], n_rows, (start - i * tq).astype(F32) * slope)
        if masked:
            kr = lax.broadcasted_iota(jnp.int32, s.shape, 0)
            qc = lax.broadcasted_iota(jnp.int32, s.shape, 1)
            qpos = (qc & (sub - 1)) + jnp.where(qc >= 2 * sub, sub, 0)
            s = jnp.where(kr <= qpos, s, neg_inf)
        s_s[pl.ds(start, n_rows), :] = s
        m_s[...] = jnp.maximum(m_s[...], jnp.max(s, axis=0, keepdims=True))

    odd = i % 2
    n_chunks = i // 2

    @pl.when(odd == 1)
    def _():
        score_chunk(0, tq, masked=False)

    def score_body(t, carry):
        score_chunk(pl.multiple_of(odd * tq + t * chunk, tq), chunk, masked=False)
        return carry

    lax.fori_loop(0, n_chunks, score_body, 0)
    diag = pl.multiple_of(i * tq, tq)
    score_chunk(diag, tq, masked=True)

    m = m_s[...]
    acc_s[...] = _dot(mvt_s[...], jnp.exp(sm_s[...] - m).astype(BF16))

    def pv_chunk(start, n_rows):
        p = jnp.exp(s_s[pl.ds(start, n_rows), :] - m).astype(BF16)
        blk = start // tq
        upd = _dot(vt_s[blk], p[:tq])
        for c in range(1, n_rows // tq):
            upd = upd + _dot(vt_s[blk + c], p[c * tq:(c + 1) * tq])
        acc_s[...] += upd

    @pl.when(odd == 1)
    def _():
        pv_chunk(0, tq)

    def pv_body(t, carry):
        pv_chunk(pl.multiple_of(odd * tq + t * chunk, tq), chunk)
        return carry

    lax.fori_loop(0, n_chunks, pv_body, 0)
    pv_chunk(diag, tq)

    lam = (jnp.exp(jnp.sum(lq1_ref[...] * lk1_ref[...], axis=1, keepdims=True))
           - jnp.exp(jnp.sum(lq2_ref[...] * lk2_ref[...], axis=1, keepdims=True)) + lam_init)
    acc = acc_s[...]
    o12 = acc[:dv] / acc[dv:dv + 1]
    for c, r0 in ((0, 0), (2 * sub, sub)):
        ot = o12[:, c:c + sub] - lam * o12[:, c + sub:c + 2 * sub]
        yt = ot * lax.rsqrt(jnp.mean(ot * ot, axis=0, keepdims=True) + NORM_EPS)
        o_ref[r0:r0 + sub, :] = (yt.T * hg_ref[...] * (1.0 - lam_init)).astype(o_ref.dtype)


def _diff_attention(proj, mk, mv, lq1, lk1, lq2, lk2, head_g, *, batch, seq, lam_init):
    n = batch * seq
    tq = ATTN_TQ
    assert seq % tq == 0
    nq = seq // tq
    dv = DA_VDIM
    n_mk = mk.shape[0]
    k_blk0 = DA_HEADS
    v_blk0 = 2 * DA_HEADS
    slopes = jnp.asarray(2.0 ** (-8.0 * (np.arange(DA_HEADS, dtype=np.float32) + 1.0) / DA_HEADS), F32)
    vec = lambda: pl.BlockSpec((1, DA_HEAD_DIM), lambda b, h, i: (0, 0))
    n_cols = 4 * ATTN_SUB
    return pl.pallas_call(
        functools.partial(_attn_kernel, lam_init=lam_init),
        grid=(batch, DA_HEADS, nq),
        in_specs=[
            pl.BlockSpec(memory_space=pltpu.SMEM),
            pl.BlockSpec((tq, dv), lambda b, h, i: (b * nq + i, h)),
            pl.BlockSpec((seq, dv), lambda b, h, i: (b, k_blk0 + h)),
            pl.BlockSpec((seq, dv), lambda b, h, i: (b, v_blk0 + h)),
            pl.BlockSpec((n_mk, dv), lambda b, h, i: (0, h)),
            pl.BlockSpec((n_mk, dv), lambda b, h, i: (0, h)),
            vec(), vec(), vec(), vec(),
            pl.BlockSpec((1, dv), lambda b, h, i: (0, 0)),
        ],
        out_specs=pl.BlockSpec((tq, dv), lambda b, h, i: (b * nq + i, h)),
        out_shape=jax.ShapeDtypeStruct((n, DA_HEADS * dv), BF16),
        scratch_shapes=[pltpu.VMEM((dv + BF16_SUBLANES, n_mk), BF16),
                        pltpu.VMEM((nq, dv + BF16_SUBLANES, tq), BF16),
                        pltpu.VMEM((ATTN_CHUNK, LANES), F32),
                        pltpu.VMEM((2 * LANES, n_cols), BF16),
                        pltpu.VMEM((seq, n_cols), F32),
                        pltpu.VMEM((n_mk, n_cols), F32),
                        pltpu.VMEM((1, n_cols), F32),
                        pltpu.VMEM((dv + BF16_SUBLANES, n_cols), F32)],
        compiler_params=pltpu.CompilerParams(
            dimension_semantics=("parallel", "parallel", "arbitrary"),
            vmem_limit_bytes=V7X_VMEM_LIMIT_BYTES),
    )(slopes, proj, proj, proj, mk, mv, lq1, lk1, lq2, lk2, head_g)


def _ret_kernel(lg_ref, q_ref, k_ref, v_ref, g_ref, mk_ref, mv_ref, o_ref, state_s, dec_s, *, chunk, k_scale):
    h = pl.program_id(1)
    ci = pl.program_id(2)
    lg = lg_ref[h]
    rows = lax.broadcasted_iota(jnp.int32, (chunk, 1), 0).astype(F32)
    end_dec = jnp.exp((chunk - 1.0 - rows) * lg)
    cross_dec = jnp.exp((rows + 1.0) * lg)
    chunk_dec = jnp.exp(jnp.full((1, 1), float(chunk), F32) * lg)
    ks = jnp.asarray(k_scale, BF16)

    @pl.when(ci == 0)
    def _():
        ri = lax.broadcasted_iota(jnp.int32, (chunk, chunk), 0)
        cc = lax.broadcasted_iota(jnp.int32, (chunk, chunk), 1)
        diff = (ri - cc).astype(F32)
        dec_s[...] = jnp.where(diff >= 0, jnp.exp(jnp.maximum(diff, 0.0) * lg), 0.0)
        kd = ((mk_ref[...] * ks).astype(F32) * end_dec).astype(BF16)
        state_s[...] = _dot_tn(kd, mv_ref[...])

    qc = q_ref[...]
    kc = k_ref[...] * ks
    vc = v_ref[...]
    a = _dot_nt(qc, kc) * dec_s[...]
    state = state_s[...]
    o = _dot(a.astype(BF16), vc) + _dot(qc, state.astype(BF16)) * cross_dec
    kd = (kc.astype(F32) * end_dec).astype(BF16)
    state_s[...] = state * chunk_dec + _dot_tn(kd, vc)
    g = g_ref[...].astype(F32)
    o_ref[...] = (_rms(o) * (g * jax.nn.sigmoid(g))).astype(o_ref.dtype)


def _retention(proj, mk, mv, *, batch, seq, chunk, d_model):
    n = batch * seq
    nc = seq // chunk
    dk = d_model // RET_HEADS
    dv = 2 * d_model // RET_HEADS
    q_blk0 = (3 * d_model) // dk
    k_blk0 = (4 * d_model) // dk
    v_blk0 = (5 * d_model) // dv
    g_blk0 = (7 * d_model) // dv
    log_gamma = jnp.asarray(np.log(1.0 - 2.0 ** (-5.0 - np.arange(RET_HEADS, dtype=np.float32))), F32)
    return pl.pallas_call(
        functools.partial(_ret_kernel, chunk=chunk, k_scale=dk ** -0.5),
        grid=(batch, RET_HEADS, nc),
        in_specs=[
            pl.BlockSpec(memory_space=pltpu.SMEM),
            pl.BlockSpec((chunk, dk), lambda b, h, c: (b * nc + c, q_blk0 + h)),
            pl.BlockSpec((chunk, dk), lambda b, h, c: (b * nc + c, k_blk0 + h)),
            pl.BlockSpec((chunk, dv), lambda b, h, c: (b * nc + c, v_blk0 + h)),
            pl.BlockSpec((chunk, dv), lambda b, h, c: (b * nc + c, g_blk0 + h)),
            pl.BlockSpec((chunk, dk), lambda b, h, c: (0, h)),
            pl.BlockSpec((chunk, dv), lambda b, h, c: (0, h)),
        ],
        out_specs=pl.BlockSpec((chunk, dv), lambda b, h, c: (b * nc + c, h)),
        out_shape=jax.ShapeDtypeStruct((n, RET_HEADS * dv), BF16),
        scratch_shapes=[pltpu.VMEM((dk, dv), F32), pltpu.VMEM((chunk, chunk), F32)],
        compiler_params=pltpu.CompilerParams(
            dimension_semantics=("parallel", "parallel", "arbitrary"),
            vmem_limit_bytes=V7X_VMEM_LIMIT_BYTES),
    )(log_gamma, proj, proj, proj, proj, mk, mv)


def _merge_kernel(x_ref, yda_ref, yret_ref, ga_ref, gr_ref, wda_ref, wret_ref, wout_ref, o_ref):
    a = _dot(yda_ref[...], wda_ref[...])
    r = _dot(yret_ref[...], wret_ref[...])
    merged = (jax.nn.sigmoid(ga_ref[...].astype(F32)) * a
              + jax.nn.sigmoid(gr_ref[...].astype(F32)) * r)
    o_ref[...] = x_ref[...] + _dot(merged.astype(BF16), wout_ref[...])


def _merge(x2, y_da, y_ret, proj, w_da, w_ret, w_out, *, tm):
    n, d = x2.shape
    ga_blk = 9
    gr_blk = 10
    const = lambda shape: pl.BlockSpec(shape, lambda i: (0, 0), pipeline_mode=pl.Buffered(1))
    return pl.pallas_call(
        _merge_kernel,
        grid=(n // tm,),
        in_specs=[
            pl.BlockSpec((tm, d), lambda i: (i, 0)),
            pl.BlockSpec((tm, y_da.shape[1]), lambda i: (i, 0)),
            pl.BlockSpec((tm, y_ret.shape[1]), lambda i: (i, 0)),
            pl.BlockSpec((tm, d), lambda i: (i, ga_blk)),
            pl.BlockSpec((tm, d), lambda i: (i, gr_blk)),
            const(w_da.shape), const(w_ret.shape), const(w_out.shape),
        ],
        out_specs=pl.BlockSpec((tm, d), lambda i: (i, 0)),
        out_shape=jax.ShapeDtypeStruct((n, d), F32),
        compiler_params=pltpu.CompilerParams(
            dimension_semantics=("parallel",),
            vmem_limit_bytes=V7X_VMEM_LIMIT_BYTES),
    )(x2, y_da, y_ret, proj, proj, w_da, w_ret, w_out)


def _mlp_kernel(h_ref, g1_ref, wup_ref, wdown_ref, g2_ref, o_ref, *, tf):
    hres = h_ref[...]
    u = (_rms(hres) * g1_ref[...]).astype(BF16)
    acc = hres
    for f in range(0, wup_ref.shape[1], tf):
        a = jnp.maximum(_dot(u, wup_ref[:, f:f + tf]), 0.0)
        acc = acc + _dot((a * a).astype(BF16), wdown_ref[f:f + tf, :])
    o_ref[...] = _rms(acc) * g2_ref[...]


def _mlp(h1, g1, w_up, w_down, g2, *, tm, tf):
    n, d = h1.shape
    const = lambda shape: pl.BlockSpec(shape, lambda i: (0, 0), pipeline_mode=pl.Buffered(1))
    return pl.pallas_call(
        functools.partial(_mlp_kernel, tf=tf),
        grid=(n // tm,),
        in_specs=[
            pl.BlockSpec((tm, d), lambda i: (i, 0)),
            pl.BlockSpec((1, d), lambda i: (0, 0)),
            const(w_up.shape), const(w_down.shape),
            pl.BlockSpec((1, d), lambda i: (0, 0)),
        ],
        out_specs=pl.BlockSpec((tm, d), lambda i: (i, 0)),
        out_shape=jax.ShapeDtypeStruct((n, d), F32),
        compiler_params=pltpu.CompilerParams(
            dimension_semantics=("parallel",),
            vmem_limit_bytes=V7X_VMEM_LIMIT_BYTES),
    )(h1, g1, w_up, w_down, g2)


def _tile(n, target):
    t = min(n, target)
    assert n % t == 0, (n, t)
    return t


def kernel(x, meta_tokens, norm_mix_g, w_in, da_lambda_q1, da_lambda_k1, da_lambda_q2, da_lambda_k2,
           da_head_g, w_proj_da, w_proj_ret, w_out, norm_mlp_g, w_up, w_down, final_norm_g):
    batch, seq, d = x.shape
    assert w_in.shape[0] == 1, "single-layer block"
    assert d // RET_HEADS == 2 * LANES and DA_HEADS * DA_VDIM == d
    n = batch * seq
    layer = 0
    lam_init = 0.8 - 0.6 * math.exp(-0.3 * layer)

    x2 = x.reshape(n, d)
    w_in_b = w_in[layer].astype(BF16)
    g_mix = norm_mix_g[layer].reshape(1, d)

    proj = _inproj(x2, g_mix, w_in_b, tm=_tile(n, 1024), tn=_tile(w_in_b.shape[1], 1024))
    proj_m = _inproj(meta_tokens.astype(x.dtype), g_mix, w_in_b, tm=N_META, tn=_tile(w_in_b.shape[1], 1024))

    pad_a = ((0, LANES - N_META), (0, 0))
    mk_a = jnp.pad(proj_m[:, d:2 * d], pad_a)
    mv_a = jnp.pad(proj_m[:, 2 * d:3 * d], pad_a)
    row = lambda v: v[layer].reshape(1, -1).astype(F32)
    y_da = _diff_attention(proj, mk_a, mv_a, row(da_lambda_q1), row(da_lambda_k1), row(da_lambda_q2),
                           row(da_lambda_k2), row(da_head_g), batch=batch, seq=seq, lam_init=lam_init)

    chunk = _tile(seq, 256)
    pad_r = ((chunk - N_META, 0), (0, 0))
    mk_r = jnp.pad(proj_m[:, 4 * d:5 * d], pad_r)
    mv_r = jnp.pad(proj_m[:, 5 * d:7 * d], pad_r)
    y_ret = _retention(proj, mk_r, mv_r, batch=batch, seq=seq, chunk=chunk, d_model=d)

    h1 = _merge(x2, y_da, y_ret, proj, w_proj_da[layer].astype(BF16), w_proj_ret[layer].astype(BF16),
                w_out[layer].astype(BF16), tm=_tile(n, 256))
    out = _mlp(h1, norm_mlp_g[layer].reshape(1, d), w_up[layer].astype(BF16), w_down[layer].astype(BF16),
               final_norm_g.reshape(1, d), tm=_tile(n, 256), tf=1024)
    return out.reshape(batch, seq, d)
```

```python
import functools
import math

import numpy as np
import jax
import jax.numpy as jnp
from jax import lax
from jax.experimental import pallas as pl
from jax.experimental.pallas import tpu as pltpu

N_META = 16
DA_HEADS = 8
DA_HEAD_DIM = 64
DA_VDIM = 2 * DA_HEAD_DIM
RET_HEADS = 4
NORM_EPS = 1e-6

F32 = jnp.float32
BF16 = jnp.bfloat16

V7X_VMEM_LIMIT_BYTES = 56 * 1024 * 1024
LANES = 128
BF16_SUBLANES = 16


def _dot(a, b):
    return jnp.dot(a, b, preferred_element_type=F32)


def _dot_nt(a, b):
    return lax.dot_general(a, b, (((1,), (1,)), ((), ())), preferred_element_type=F32)


def _dot_tn(a, b):
    return lax.dot_general(a, b, (((0,), (0,)), ((), ())), preferred_element_type=F32)


def _rms(xf):
    return xf * lax.rsqrt(jnp.mean(xf * xf, axis=-1, keepdims=True) + NORM_EPS)


def _inproj_kernel(x_ref, g_ref, w_ref, o_ref, u_ref):
    @pl.when(pl.program_id(1) == 0)
    def _():
        u_ref[...] = (_rms(x_ref[...]) * g_ref[...]).astype(BF16)

    o_ref[...] = _dot(u_ref[...], w_ref[...]).astype(o_ref.dtype)


def _inproj_t_kernel(x_ref, g_ref, w_ref, wt_ref, o_ref, ot_ref, u_ref):
    @pl.when(pl.program_id(1) == 0)
    def _():
        u = (_rms(x_ref[...]) * g_ref[...]).astype(BF16)
        u_ref[...] = u
        ot_ref[...] = _dot_nt(wt_ref[...], u).astype(ot_ref.dtype)

    o_ref[...] = _dot(u_ref[...], w_ref[...]).astype(o_ref.dtype)


def _inproj(x2, g, w, *, tm, tn):
    n, d = x2.shape
    d_in = w.shape[1]
    return pl.pallas_call(
        _inproj_kernel,
        grid=(n // tm, d_in // tn),
        in_specs=[
            pl.BlockSpec((tm, d), lambda i, j: (i, 0)),
            pl.BlockSpec((1, d), lambda i, j: (0, 0)),
            pl.BlockSpec((d, tn), lambda i, j: (0, j)),
        ],
        out_specs=pl.BlockSpec((tm, tn), lambda i, j: (i, j)),
        out_shape=jax.ShapeDtypeStruct((n, d_in), BF16),
        scratch_shapes=[pltpu.VMEM((tm, d), BF16)],
        compiler_params=pltpu.CompilerParams(
            dimension_semantics=("parallel", "arbitrary"),
            vmem_limit_bytes=V7X_VMEM_LIMIT_BYTES),
    )(x2, g, w)


def _inproj_t(x2, g, w, wt, *, tm, tn):
    n, d = x2.shape
    d_in = w.shape[1]
    d_t = wt.shape[0]
    return pl.pallas_call(
        _inproj_t_kernel,
        grid=(n // tm, d_in // tn),
        in_specs=[
            pl.BlockSpec((tm, d), lambda i, j: (i, 0)),
            pl.BlockSpec((1, d), lambda i, j: (0, 0)),
            pl.BlockSpec((d, tn), lambda i, j: (0, j)),
            pl.BlockSpec((d_t, d), lambda i, j: (0, 0), pipeline_mode=pl.Buffered(1)),
        ],
        out_specs=[pl.BlockSpec((tm, tn), lambda i, j: (i, j)),
                   pl.BlockSpec((d_t, tm), lambda i, j: (0, i))],
        out_shape=[jax.ShapeDtypeStruct((n, d_in), BF16), jax.ShapeDtypeStruct((d_t, n), BF16)],
        scratch_shapes=[pltpu.VMEM((tm, d), BF16)],
        compiler_params=pltpu.CompilerParams(
            dimension_semantics=("parallel", "arbitrary"),
            vmem_limit_bytes=V7X_VMEM_LIMIT_BYTES),
    )(x2, g, w, wt)


ATTN_TQ = 512
ATTN_SUB = 256
ATTN_CHUNK = 1024
N_FEAT = 4


def _attn_kernel(slopes_ref, qt_ref, k_ref, vt_ref, mk_ref, mvt_ref, lq1_ref, lk1_ref, lq2_ref, lk2_ref,
                 hg_ref, o_ref, mvt_s, vt_s, kpos_s, qt_s, s_s, sb_s, sm_s, m_s, acc_s, *, lam_init):
    h = pl.program_id(1)
    i = pl.program_id(2)
    slope = slopes_ref[h]
    tq, sub, chunk = ATTN_TQ, ATTN_SUB, ATTN_CHUNK
    dv = vt_ref.shape[0]
    n_cols = 4 * sub
    neg_inf = jnp.float32(-jnp.inf)
    lane_row = lax.broadcasted_iota(jnp.int32, (1, LANES), 1)

    @pl.when(i == 0)
    def _():
        n_ones = vt_s.shape[1] - dv
        for c in range(vt_s.shape[0]):
            vt_s[c, :dv, :] = vt_ref[:, c * tq:(c + 1) * tq]
            vt_s[c, dv:, :] = jnp.ones((n_ones, tq), vt_s.dtype)
        mvt_s[:dv, :] = mvt_ref[...]
        mvt_s[dv:, :] = jnp.ones((n_ones, LANES), mvt_s.dtype)
        pos = lax.broadcasted_iota(jnp.int32, (chunk, LANES), 0).astype(F32) * slope
        plane = lax.broadcasted_iota(jnp.int32, (chunk, LANES), 1)
        pos_hi = pos.astype(BF16).astype(F32)
        kpos_s[...] = jnp.where(plane == 0, pos_hi, jnp.where(plane == 1, pos - pos_hi, 0.0))
        frow = lax.broadcasted_iota(jnp.int32, (LANES, n_cols), 0)
        qt_s[LANES:, :] = jnp.where(frow < N_FEAT, 1.0, 0.0).astype(qt_s.dtype)

    qf = qt_ref[...].astype(F32) * (DA_HEAD_DIM ** -0.5)
    frow = lax.broadcasted_iota(jnp.int32, (LANES, sub), 0)
    groups = []
    for half in (qf[:, :sub], qf[:, sub:]):
        groups.append(jnp.where(frow < DA_HEAD_DIM, half, 0.0))
        groups.append(jnp.where(frow >= DA_HEAD_DIM, half, 0.0))
    qt_s[:LANES, :] = jnp.concatenate(groups, axis=1).astype(qt_s.dtype)

    def key_feats(row0, n_rows, offset):
        off = jnp.full((1, LANES), offset, F32)
        off_hi = off.astype(BF16).astype(F32)
        off_row = jnp.where(lane_row == 2, off_hi, jnp.where(lane_row == 3, off - off_hi, 0.0))
        return (kpos_s[row0:row0 + n_rows, :] + off_row).astype(BF16)

    def with_feats(k_blk, row0, offset):
        return jnp.concatenate([k_blk, key_feats(row0, k_blk.shape[0], offset)], axis=1)

    diag = pl.multiple_of(i * tq, tq)
    lhs_a = jnp.concatenate(
        [with_feats(k_ref[pl.ds(diag, sub), :], 0, 0.0),
         with_feats(mk_ref[...], 0, -(i * tq + N_META).astype(F32) * slope)], axis=0)
    s_a = _dot(lhs_a, qt_s[...])
    s_b = _dot(with_feats(k_ref[pl.ds(diag + sub, sub), :], sub, 0.0), qt_s[:, 2 * sub:])
    kr = lax.broadcasted_iota(jnp.int32, (sub, 2 * sub), 0)
    qc = lax.broadcasted_iota(jnp.int32, (sub, 2 * sub), 1)
    causal = kr <= (qc & (sub - 1))
    s_top = jnp.concatenate([jnp.where(causal, s_a[:sub, :2 * sub], neg_inf), s_a[:sub, 2 * sub:]], axis=1)
    s_b = jnp.where(causal, s_b, neg_inf)
    s_s[pl.ds(diag, sub), :] = s_top
    sb_s[...] = s_b
    sm_s[...] = s_a[sub:]
    m_a = jnp.maximum(jnp.max(s_top, axis=0, keepdims=True), jnp.max(s_a[sub:], axis=0, keepdims=True))
    m_b = jnp.max(s_b, axis=0, keepdims=True)
    m_s[...] = jnp.concatenate([m_a[:, :2 * sub], jnp.maximum(m_a[:, 2 * sub:], m_b)], axis=1)

    def score_chunk(start, n_rows):
        s = _dot(with_feats(k_ref[pl.ds(start, n_rows), :], 0, (start - i * tq).astype(F32) * slope),
                 qt_s[...])
        s_s[pl.ds(start, n_rows), :] = s
        m_s[...] = jnp.maximum(m_s[...], jnp.max(s, axis=0, keepdims=True))

    odd = i % 2
    n_chunks = i // 2

    @pl.when(odd == 1)
    def _():
        score_chunk(0, tq)

    def score_body(t, carry):
        score_chunk(pl.multiple_of(odd * tq + t * chunk, tq), chunk)
        return carry

    lax.fori_loop(0, n_chunks, score_body, 0)

    m = m_s[...]
    vt_i = vt_s[i]
    p_a = jnp.concatenate(
        [jnp.exp(s_s[pl.ds(diag, sub), :] - m).astype(BF16),
         jnp.exp(sm_s[...] - m).astype(BF16),
         jnp.zeros((LANES - N_META, n_cols), BF16)], axis=0)
    acc_a = _dot(jnp.concatenate([vt_i[:, :sub], mvt_s[...]], axis=1), p_a)
    acc_b = _dot(vt_i[:, sub:], jnp.exp(sb_s[...] - m[:, 2 * sub:]).astype(BF16))
    acc_s[...] = jnp.concatenate([acc_a[:, :2 * sub], acc_a[:, 2 * sub:] + acc_b], axis=1)

    def pv_chunk(start, n_rows):
        p = jnp.exp(s_s[pl.ds(start, n_rows), :] - m).astype(BF16)
        blk = start // tq
        upd = _dot(vt_s[blk], p[:tq])
        for c in range(1, n_rows // tq):
            upd = upd + _dot(vt_s[blk + c], p[c * tq:(c + 1) * tq])
        acc_s[...] += upd

    @pl.when(odd == 1)
    def _():
        pv_chunk(0, tq)

    def pv_body(t, carry):
        pv_chunk(pl.multiple_of(odd * tq + t * chunk, tq), chunk)
        return carry

    lax.fori_loop(0, n_chunks, pv_body, 0)

    lam = (jnp.exp(jnp.sum(lq1_ref[...] * lk1_ref[...], axis=1, keepdims=True))
           - jnp.exp(jnp.sum(lq2_ref[...] * lk2_ref[...], axis=1, keepdims=True)) + lam_init)
    acc = acc_s[...]
    o12 = acc[:dv] * (1.0 / acc[dv:dv + 1])
    for c, r0 in ((0, 0), (2 * sub, sub)):
        ot = o12[:, c:c + sub] - lam * o12[:, c + sub:c + 2 * sub]
        yt = ot * lax.rsqrt(jnp.mean(ot * ot, axis=0, keepdims=True) + NORM_EPS)
        o_ref[r0:r0 + sub, :] = (yt.T * hg_ref[...] * (1.0 - lam_init)).astype(o_ref.dtype)


def _diff_attention(proj, proj_t, mk, mvt, lq1, lk1, lq2, lk2, head_g, *, batch, seq, lam_init):
    n = batch * seq
    tq = ATTN_TQ
    assert seq % tq == 0
    nq = seq // tq
    dv = DA_VDIM
    slopes = jnp.asarray(2.0 ** (-8.0 * (np.arange(DA_HEADS, dtype=np.float32) + 1.0) / DA_HEADS), F32)
    vec = lambda: pl.BlockSpec((1, DA_HEAD_DIM), lambda b, h, i: (0, 0))
    n_cols = 4 * ATTN_SUB
    n_vrows = dv + BF16_SUBLANES
    return pl.pallas_call(
        functools.partial(_attn_kernel, lam_init=lam_init),
        grid=(batch, DA_HEADS, nq),
        in_specs=[
            pl.BlockSpec(memory_space=pltpu.SMEM),
            pl.BlockSpec((dv, tq), lambda b, h, i: (h, b * nq + i)),
            pl.BlockSpec((seq, dv), lambda b, h, i: (b, h)),
            pl.BlockSpec((dv, seq), lambda b, h, i: (DA_HEADS + h, b)),
            pl.BlockSpec((N_META, dv), lambda b, h, i: (0, h)),
            pl.BlockSpec((dv, LANES), lambda b, h, i: (h, 0)),
            vec(), vec(), vec(), vec(),
            pl.BlockSpec((1, dv), lambda b, h, i: (0, 0)),
        ],
        out_specs=pl.BlockSpec((tq, dv), lambda b, h, i: (b * nq + i, h)),
        out_shape=jax.ShapeDtypeStruct((n, DA_HEADS * dv), BF16),
        scratch_shapes=[pltpu.VMEM((n_vrows, LANES), BF16),
                        pltpu.VMEM((nq, n_vrows, tq), BF16),
                        pltpu.VMEM((ATTN_CHUNK, LANES), F32),
                        pltpu.VMEM((2 * LANES, n_cols), BF16),
                        pltpu.VMEM((seq, n_cols), F32),
                        pltpu.VMEM((ATTN_SUB, 2 * ATTN_SUB), F32),
                        pltpu.VMEM((N_META, n_cols), F32),
                        pltpu.VMEM((1, n_cols), F32),
                        pltpu.VMEM((n_vrows, n_cols), F32)],
        compiler_params=pltpu.CompilerParams(
            dimension_semantics=("parallel", "parallel", "arbitrary"),
            vmem_limit_bytes=V7X_VMEM_LIMIT_BYTES),
    )(slopes, proj_t, proj, proj_t, mk, mvt, lq1, lk1, lq2, lk2, head_g)


def _ret_kernel(lg_ref, q_ref, k_ref, v_ref, g_ref, mk_ref, mv_ref, o_ref, state_s, dec_s, *, chunk, k_scale):
    h = pl.program_id(1)
    ci = pl.program_id(2)
    lg = lg_ref[h]
    rows = lax.broadcasted_iota(jnp.int32, (chunk, 1), 0).astype(F32)
    end_dec = jnp.exp((chunk - 1.0 - rows) * lg)
    cross_dec = jnp.exp((rows + 1.0) * lg)
    chunk_dec = jnp.exp(jnp.full((1, 1), float(chunk), F32) * lg)
    ks = jnp.asarray(k_scale, BF16)

    @pl.when(ci == 0)
    def _():
        ri = lax.broadcasted_iota(jnp.int32, (chunk, chunk), 0)
        cc = lax.broadcasted_iota(jnp.int32, (chunk, chunk), 1)
        diff = (ri - cc).astype(F32)
        dec_s[...] = jnp.where(diff >= 0, jnp.exp(jnp.maximum(diff, 0.0) * lg), 0.0)
        kd = ((mk_ref[...] * ks).astype(F32) * end_dec).astype(BF16)
        state_s[...] = _dot_tn(kd, mv_ref[...])

    state = state_s[...]
    for r0 in range(0, q_ref.shape[0], chunk):
        rs = slice(r0, r0 + chunk)
        qc = q_ref[rs, :]
        kc = k_ref[rs, :] * ks
        vc = v_ref[rs, :]
        a = _dot_nt(qc, kc) * dec_s[...]
        o = _dot(a.astype(BF16), vc) + _dot(qc, state.astype(BF16)) * cross_dec
        kd = (kc.astype(F32) * end_dec).astype(BF16)
        state = state * chunk_dec + _dot_tn(kd, vc)
        g = g_ref[rs, :].astype(F32)
        o_ref[rs, :] = (_rms(o) * (g * jax.nn.sigmoid(g))).astype(o_ref.dtype)
    state_s[...] = state


def _retention(proj, mk, mv, *, batch, seq, chunk, step_rows, d_model):
    n = batch * seq
    nc = seq // step_rows
    dk = d_model // RET_HEADS
    dv = 2 * d_model // RET_HEADS
    q_blk0 = (1 * d_model) // dk
    k_blk0 = (2 * d_model) // dk
    v_blk0 = (3 * d_model) // dv
    g_blk0 = (5 * d_model) // dv
    log_gamma = jnp.asarray(np.log(1.0 - 2.0 ** (-5.0 - np.arange(RET_HEADS, dtype=np.float32))), F32)
    return pl.pallas_call(
        functools.partial(_ret_kernel, chunk=chunk, k_scale=dk ** -0.5),
        grid=(batch, RET_HEADS, nc),
        in_specs=[
            pl.BlockSpec(memory_space=pltpu.SMEM),
            pl.BlockSpec((step_rows, dk), lambda b, h, c: (b * nc + c, q_blk0 + h)),
            pl.BlockSpec((step_rows, dk), lambda b, h, c: (b * nc + c, k_blk0 + h)),
            pl.BlockSpec((step_rows, dv), lambda b, h, c: (b * nc + c, v_blk0 + h)),
            pl.BlockSpec((step_rows, dv), lambda b, h, c: (b * nc + c, g_blk0 + h)),
            pl.BlockSpec((chunk, dk), lambda b, h, c: (0, h)),
            pl.BlockSpec((chunk, dv), lambda b, h, c: (0, h)),
        ],
        out_specs=pl.BlockSpec((step_rows, dv), lambda b, h, c: (b * nc + c, h)),
        out_shape=jax.ShapeDtypeStruct((n, RET_HEADS * dv), BF16),
        scratch_shapes=[pltpu.VMEM((dk, dv), F32), pltpu.VMEM((chunk, chunk), F32)],
        compiler_params=pltpu.CompilerParams(
            dimension_semantics=("parallel", "parallel", "arbitrary"),
            vmem_limit_bytes=V7X_VMEM_LIMIT_BYTES),
    )(log_gamma, proj, proj, proj, proj, mk, mv)


def _merge_kernel(x_ref, yda_ref, yret_ref, ga_ref, gr_ref, wda_ref, wret_ref, wout_ref, o_ref):
    a = _dot(yda_ref[...], wda_ref[...])
    r = _dot(yret_ref[...], wret_ref[...])
    merged = (jax.nn.sigmoid(ga_ref[...].astype(F32)) * a
              + jax.nn.sigmoid(gr_ref[...].astype(F32)) * r)
    o_ref[...] = x_ref[...] + _dot(merged.astype(BF16), wout_ref[...])


def _merge(x2, y_da, y_ret, proj, w_da, w_ret, w_out, *, tm):
    n, d = x2.shape
    ga_blk = 7
    gr_blk = 8
    const = lambda shape: pl.BlockSpec(shape, lambda i: (0, 0), pipeline_mode=pl.Buffered(1))
    return pl.pallas_call(
        _merge_kernel,
        grid=(n // tm,),
        in_specs=[
            pl.BlockSpec((tm, d), lambda i: (i, 0)),
            pl.BlockSpec((tm, y_da.shape[1]), lambda i: (i, 0)),
            pl.BlockSpec((tm, y_ret.shape[1]), lambda i: (i, 0)),
            pl.BlockSpec((tm, d), lambda i: (i, ga_blk)),
            pl.BlockSpec((tm, d), lambda i: (i, gr_blk)),
            const(w_da.shape), const(w_ret.shape), const(w_out.shape),
        ],
        out_specs=pl.BlockSpec((tm, d), lambda i: (i, 0)),
        out_shape=jax.ShapeDtypeStruct((n, d), F32),
        compiler_params=pltpu.CompilerParams(
            dimension_semantics=("parallel",),
            vmem_limit_bytes=V7X_VMEM_LIMIT_BYTES),
    )(x2, y_da, y_ret, proj, proj, w_da, w_ret, w_out)


def _mlp_kernel(h_ref, g1_ref, wup_ref, wdown_ref, g2_ref, o_ref, *, tf):
    hres = h_ref[...]
    u = (_rms(hres) * g1_ref[...]).astype(BF16)
    acc = hres
    for f in range(0, wup_ref.shape[1], tf):
        a = jnp.maximum(_dot(u, wup_ref[:, f:f + tf]), 0.0)
        acc = acc + _dot((a * a).astype(BF16), wdown_ref[f:f + tf, :])
    o_ref[...] = _rms(acc) * g2_ref[...]


def _mlp(h1, g1, w_up, w_down, g2, *, tm, tf):
    n, d = h1.shape
    const = lambda shape: pl.BlockSpec(shape, lambda i: (0, 0), pipeline_mode=pl.Buffered(1))
    return pl.pallas_call(
        functools.partial(_mlp_kernel, tf=tf),
        grid=(n // tm,),
        in_specs=[
            pl.BlockSpec((tm, d), lambda i: (i, 0)),
            pl.BlockSpec((1, d), lambda i: (0, 0)),
            const(w_up.shape), const(w_down.shape),
            pl.BlockSpec((1, d), lambda i: (0, 0)),
        ],
        out_specs=pl.BlockSpec((tm, d), lambda i: (i, 0)),
        out_shape=jax.ShapeDtypeStruct((n, d), F32),
        compiler_params=pltpu.CompilerParams(
            dimension_semantics=("parallel",),
            vmem_limit_bytes=V7X_VMEM_LIMIT_BYTES),
    )(h1, g1, w_up, w_down, g2)


def _tile(n, target):
    t = min(n, target)
    assert n % t == 0, (n, t)
    return t


def kernel(x, meta_tokens, norm_mix_g, w_in, da_lambda_q1, da_lambda_k1, da_lambda_q2, da_lambda_k2,
           da_head_g, w_proj_da, w_proj_ret, w_out, norm_mlp_g, w_up, w_down, final_norm_g):
    batch, seq, d = x.shape
    assert w_in.shape[0] == 1, "single-layer block"
    assert d // RET_HEADS == 2 * LANES and DA_HEADS * DA_VDIM == d
    n = batch * seq
    layer = 0
    lam_init = 0.8 - 0.6 * math.exp(-0.3 * layer)

    x2 = x.reshape(n, d)
    w_in_b = w_in[layer].astype(BF16)
    g_mix = norm_mix_g[layer].reshape(1, d)

    w_rows = jnp.concatenate([w_in_b[:, d:2 * d], w_in_b[:, 3 * d:]], axis=1)
    w_feat = jnp.concatenate([w_in_b[:, :d], w_in_b[:, 2 * d:3 * d]], axis=1).T
    proj, proj_t = _inproj_t(x2, g_mix, w_rows, w_feat, tm=_tile(n, 1024), tn=_tile(w_rows.shape[1], 1024))
    proj_m = _inproj(meta_tokens.astype(x.dtype), g_mix, w_in_b, tm=N_META, tn=_tile(w_in_b.shape[1], 1024))

    mvt_a = jnp.pad(proj_m[:, 2 * d:3 * d].T, ((0, 0), (0, LANES - N_META)))
    row = lambda v: v[layer].reshape(1, -1).astype(F32)
    y_da = _diff_attention(proj, proj_t, proj_m[:, d:2 * d], mvt_a, row(da_lambda_q1), row(da_lambda_k1),
                           row(da_lambda_q2), row(da_lambda_k2), row(da_head_g), batch=batch, seq=seq,
                           lam_init=lam_init)

    chunk = _tile(seq, 256)
    pad_r = ((chunk - N_META, 0), (0, 0))
    mk_r = jnp.pad(proj_m[:, 4 * d:5 * d], pad_r)
    mv_r = jnp.pad(proj_m[:, 5 * d:7 * d], pad_r)
    y_ret = _retention(proj, mk_r, mv_r, batch=batch, seq=seq, chunk=chunk, step_rows=_tile(seq, 1024),
                       d_model=d)

    h1 = _merge(x2, y_da, y_ret, proj, w_proj_da[layer].astype(BF16), w_proj_ret[layer].astype(BF16),
                w_out[layer].astype(BF16), tm=_tile(n, 256))
    out = _mlp(h1, norm_mlp_g[layer].reshape(1, d), w_up[layer].astype(BF16), w_down[layer].astype(BF16),
               final_norm_g.reshape(1, d), tm=_tile(n, 256), tf=1024)
    return out.reshape(batch, seq, d)
```

```python
import functools
import math

import numpy as np
import jax
import jax.numpy as jnp
from jax import lax
from jax.experimental import pallas as pl
from jax.experimental.pallas import tpu as pltpu

N_META = 16
DA_HEADS = 8
DA_HEAD_DIM = 64
DA_VDIM = 2 * DA_HEAD_DIM
RET_HEADS = 4
NORM_EPS = 1e-6

F32 = jnp.float32
BF16 = jnp.bfloat16

V7X_VMEM_LIMIT_BYTES = 56 * 1024 * 1024
LANES = 128
BF16_SUBLANES = 16


def _dot(a, b):
    return jnp.dot(a, b, preferred_element_type=F32)


def _dot_nt(a, b):
    return lax.dot_general(a, b, (((1,), (1,)), ((), ())), preferred_element_type=F32)


def _dot_tn(a, b):
    return lax.dot_general(a, b, (((0,), (0,)), ((), ())), preferred_element_type=F32)


def _rms(xf):
    return xf * lax.rsqrt(jnp.mean(xf * xf, axis=-1, keepdims=True) + NORM_EPS)


def _inproj_kernel(x_ref, g_ref, w_ref, o_ref, u_ref):
    @pl.when(pl.program_id(1) == 0)
    def _():
        u_ref[...] = (_rms(x_ref[...]) * g_ref[...]).astype(BF16)

    o_ref[...] = _dot(u_ref[...], w_ref[...]).astype(o_ref.dtype)


def _inproj_t_kernel(x_ref, g_ref, w_ref, wt_ref, o_ref, ot_ref, u_ref):
    @pl.when(pl.program_id(1) == 0)
    def _():
        u = (_rms(x_ref[...]) * g_ref[...]).astype(BF16)
        u_ref[...] = u
        ot_ref[...] = _dot_nt(wt_ref[...], u).astype(ot_ref.dtype)

    o_ref[...] = _dot(u_ref[...], w_ref[...]).astype(o_ref.dtype)


def _inproj(x2, g, w, *, tm, tn):
    n, d = x2.shape
    d_in = w.shape[1]
    return pl.pallas_call(
        _inproj_kernel,
        grid=(n // tm, d_in // tn),
        in_specs=[
            pl.BlockSpec((tm, d), lambda i, j: (i, 0)),
            pl.BlockSpec((1, d), lambda i, j: (0, 0)),
            pl.BlockSpec((d, tn), lambda i, j: (0, j)),
        ],
        out_specs=pl.BlockSpec((tm, tn), lambda i, j: (i, j)),
        out_shape=jax.ShapeDtypeStruct((n, d_in), BF16),
        scratch_shapes=[pltpu.VMEM((tm, d), BF16)],
        compiler_params=pltpu.CompilerParams(
            dimension_semantics=("parallel", "arbitrary"),
            vmem_limit_bytes=V7X_VMEM_LIMIT_BYTES),
    )(x2, g, w)


def _inproj_t(x2, g, w, wt, *, tm, tn):
    n, d = x2.shape
    d_in = w.shape[1]
    d_t = wt.shape[0]
    return pl.pallas_call(
        _inproj_t_kernel,
        grid=(n // tm, d_in // tn),
        in_specs=[
            pl.BlockSpec((tm, d), lambda i, j: (i, 0)),
            pl.BlockSpec((1, d), lambda i, j: (0, 0)),
            pl.BlockSpec((d, tn), lambda i, j: (0, j)),
            pl.BlockSpec((d_t, d), lambda i, j: (0, 0), pipeline_mode=pl.Buffered(1)),
        ],
        out_specs=[pl.BlockSpec((tm, tn), lambda i, j: (i, j)),
                   pl.BlockSpec((d_t, tm), lambda i, j: (0, i))],
        out_shape=[jax.ShapeDtypeStruct((n, d_in), BF16), jax.ShapeDtypeStruct((d_t, n), BF16)],
        scratch_shapes=[pltpu.VMEM((tm, d), BF16)],
        compiler_params=pltpu.CompilerParams(
            dimension_semantics=("parallel", "arbitrary"),
            vmem_limit_bytes=V7X_VMEM_LIMIT_BYTES),
    )(x2, g, w, wt)


ATTN_TQ = 1024
ATTN_SUB = 256
ATTN_CHUNK = 1024
ATTN_VT_COLS = 512
N_FEAT = 4


def _attn_kernel(slopes_ref, qt_ref, k_ref, vt_ref, mk_ref, mvt_ref, lq1_ref, lk1_ref, lq2_ref, lk2_ref,
                 hg_ref, o_ref, mvt_s, vt_s, kpos_s, qt_s, s_s, sm_s, m_s, acc_s, *, lam_init):
    h = pl.program_id(1)
    i = pl.program_id(2)
    slope = slopes_ref[h]
    tq, sub, chunk, vtc = ATTN_TQ, ATTN_SUB, ATTN_CHUNK, ATTN_VT_COLS
    n_sub = tq // sub
    dv = vt_ref.shape[0]
    n_cols = 2 * tq
    neg_inf = jnp.float32(-jnp.inf)
    lane_row = lax.broadcasted_iota(jnp.int32, (1, LANES), 1)

    @pl.when(i == 0)
    def _():
        n_ones = vt_s.shape[1] - dv
        for c in range(vt_s.shape[0]):
            vt_s[c, :dv, :] = vt_ref[:, c * vtc:(c + 1) * vtc]
            vt_s[c, dv:, :] = jnp.ones((n_ones, vtc), vt_s.dtype)
        mvt_s[:dv, :] = mvt_ref[...]
        mvt_s[dv:, :] = jnp.ones((n_ones, LANES), mvt_s.dtype)
        pos = lax.broadcasted_iota(jnp.int32, kpos_s.shape, 0).astype(F32) * slope
        plane = lax.broadcasted_iota(jnp.int32, kpos_s.shape, 1)
        pos_hi = pos.astype(BF16).astype(F32)
        kpos_s[...] = jnp.where(plane == 0, pos_hi, jnp.where(plane == 1, pos - pos_hi, 0.0))
        frow = lax.broadcasted_iota(jnp.int32, (LANES, n_cols), 0)
        qt_s[LANES:, :] = jnp.where(frow < N_FEAT, 1.0, 0.0).astype(qt_s.dtype)

    frow = lax.broadcasted_iota(jnp.int32, (LANES, sub), 0)
    for c in range(n_sub):
        qf = qt_ref[:, c * sub:(c + 1) * sub].astype(F32) * (DA_HEAD_DIM ** -0.5)
        qt_s[:LANES, 2 * c * sub:(2 * c + 1) * sub] = jnp.where(frow < DA_HEAD_DIM, qf, 0.0).astype(qt_s.dtype)
        qt_s[:LANES, (2 * c + 1) * sub:(2 * c + 2) * sub] = (
            jnp.where(frow >= DA_HEAD_DIM, qf, 0.0).astype(qt_s.dtype))

    def key_feats(row0, n_rows, offset):
        off = jnp.full((1, LANES), offset, F32)
        off_hi = off.astype(BF16).astype(F32)
        off_row = jnp.where(lane_row == 2, off_hi, jnp.where(lane_row == 3, off - off_hi, 0.0))
        return (kpos_s[row0:row0 + n_rows, :] + off_row).astype(BF16)

    def with_feats(k_blk, row0, offset):
        return jnp.concatenate([k_blk, key_feats(row0, k_blk.shape[0], offset)], axis=1)

    diag = pl.multiple_of(i * tq, tq)
    kr = lax.broadcasted_iota(jnp.int32, (sub, 2 * sub), 0)
    qc = lax.broadcasted_iota(jnp.int32, (sub, 2 * sub), 1)
    causal = kr <= (qc & (sub - 1))
    for r in range(n_sub):
        c0 = 2 * sub * r
        lhs = with_feats(k_ref[pl.ds(diag + r * sub, sub), :], r * sub, 0.0)
        if r == 0:
            lhs = jnp.concatenate(
                [lhs, with_feats(mk_ref[...], 0, -(i * tq + N_META).astype(F32) * slope)], axis=0)
        s = _dot(lhs, qt_s[:, c0:])
        s_first = jnp.where(causal, s[:sub, :2 * sub], neg_inf)
        s_top = s_first if r == n_sub - 1 else jnp.concatenate([s_first, s[:sub, 2 * sub:]], axis=1)
        s_s[pl.ds(diag + r * sub, sub), c0:] = s_top
        m_r = jnp.max(s_top, axis=0, keepdims=True)
        if r == 0:
            sm_s[...] = s[sub:]
            m_s[...] = jnp.maximum(m_r, jnp.max(s[sub:], axis=0, keepdims=True))
        else:
            m_s[:, c0:] = jnp.maximum(m_s[:, c0:], m_r)

    def score_body(t, carry):
        start = pl.multiple_of(t * chunk, chunk)
        s = _dot(with_feats(k_ref[pl.ds(start, chunk), :], 0, (start - i * tq).astype(F32) * slope),
                 qt_s[...])
        s_s[pl.ds(start, chunk), :] = s
        m_s[...] = jnp.maximum(m_s[...], jnp.max(s, axis=0, keepdims=True))
        return carry

    n_chunks = i * (tq // chunk)
    lax.fori_loop(0, n_chunks, score_body, 0)

    m = m_s[...]
    for r in range(n_sub):
        c0 = 2 * sub * r
        vt_r = vt_s[(diag + r * sub) // vtc][:, (r * sub) % vtc:(r * sub) % vtc + sub]
        p = jnp.exp(s_s[pl.ds(diag + r * sub, sub), c0:] - m[:, c0:]).astype(BF16)
        if r == 0:
            p = jnp.concatenate([p, jnp.exp(sm_s[...] - m).astype(BF16),
                                 jnp.zeros((LANES - N_META, n_cols), BF16)], axis=0)
            acc_s[...] = _dot(jnp.concatenate([vt_r, mvt_s[...]], axis=1), p)
        else:
            acc_s[:, c0:] += _dot(vt_r, p)

    def pv_body(t, carry):
        start = pl.multiple_of(t * chunk, chunk)
        p = jnp.exp(s_s[pl.ds(start, chunk), :] - m).astype(BF16)
        blk = start // vtc
        upd = _dot(vt_s[blk], p[:vtc])
        for c in range(1, chunk // vtc):
            upd = upd + _dot(vt_s[blk + c], p[c * vtc:(c + 1) * vtc])
        acc_s[...] += upd
        return carry

    lax.fori_loop(0, n_chunks, pv_body, 0)

    lam = (jnp.exp(jnp.sum(lq1_ref[...] * lk1_ref[...], axis=1, keepdims=True))
           - jnp.exp(jnp.sum(lq2_ref[...] * lk2_ref[...], axis=1, keepdims=True)) + lam_init)
    inv_l = 1.0 / acc_s[dv:dv + 1, :]
    for c in range(n_sub):
        c0 = 2 * sub * c
        o1 = acc_s[:dv, c0:c0 + sub] * inv_l[:, c0:c0 + sub]
        o2 = acc_s[:dv, c0 + sub:c0 + 2 * sub] * inv_l[:, c0 + sub:c0 + 2 * sub]
        ot = o1 - lam * o2
        yt = ot * lax.rsqrt(jnp.mean(ot * ot, axis=0, keepdims=True) + NORM_EPS)
        o_ref[c * sub:(c + 1) * sub, :] = (yt.T * hg_ref[...] * (1.0 - lam_init)).astype(o_ref.dtype)


def _diff_attention(proj, proj_t, mk, mvt, lq1, lk1, lq2, lk2, head_g, *, batch, seq, lam_init):
    n = batch * seq
    tq = ATTN_TQ
    assert seq % tq == 0 and tq % ATTN_CHUNK == 0 and ATTN_CHUNK % ATTN_VT_COLS == 0
    nq = seq // tq
    dv = DA_VDIM
    slopes = jnp.asarray(2.0 ** (-8.0 * (np.arange(DA_HEADS, dtype=np.float32) + 1.0) / DA_HEADS), F32)
    vec = lambda: pl.BlockSpec((1, DA_HEAD_DIM), lambda b, h, i: (0, 0))
    n_cols = 2 * tq
    n_vrows = dv + BF16_SUBLANES
    return pl.pallas_call(
        functools.partial(_attn_kernel, lam_init=lam_init),
        grid=(batch, DA_HEADS, nq),
        in_specs=[
            pl.BlockSpec(memory_space=pltpu.SMEM),
            pl.BlockSpec((dv, tq), lambda b, h, i: (h, b * nq + i)),
            pl.BlockSpec((seq, dv), lambda b, h, i: (b, h)),
            pl.BlockSpec((dv, seq), lambda b, h, i: (DA_HEADS + h, b)),
            pl.BlockSpec((N_META, dv), lambda b, h, i: (0, h)),
            pl.BlockSpec((dv, LANES), lambda b, h, i: (h, 0)),
            vec(), vec(), vec(), vec(),
            pl.BlockSpec((1, dv), lambda b, h, i: (0, 0)),
        ],
        out_specs=pl.BlockSpec((tq, dv), lambda b, h, i: (b * nq + i, h)),
        out_shape=jax.ShapeDtypeStruct((n, DA_HEADS * dv), BF16),
        scratch_shapes=[pltpu.VMEM((n_vrows, LANES), BF16),
                        pltpu.VMEM((seq // ATTN_VT_COLS, n_vrows, ATTN_VT_COLS), BF16),
                        pltpu.VMEM((max(tq, ATTN_CHUNK), LANES), F32),
                        pltpu.VMEM((2 * LANES, n_cols), BF16),
                        pltpu.VMEM((seq, n_cols), F32),
                        pltpu.VMEM((N_META, n_cols), F32),
                        pltpu.VMEM((1, n_cols), F32),
                        pltpu.VMEM((n_vrows, n_cols), F32)],
        compiler_params=pltpu.CompilerParams(
            dimension_semantics=("parallel", "parallel", "arbitrary"),
            vmem_limit_bytes=V7X_VMEM_LIMIT_BYTES),
    )(slopes, proj_t, proj, proj_t, mk, mvt, lq1, lk1, lq2, lk2, head_g)


def _ret_kernel(lg_ref, q_ref, k_ref, v_ref, g_ref, mk_ref, mv_ref, o_ref, state_s, dec_s, *, chunk, k_scale):
    h = pl.program_id(1)
    ci = pl.program_id(2)
    lg = lg_ref[h]
    rows = lax.broadcasted_iota(jnp.int32, (chunk, 1), 0).astype(F32)
    end_dec = jnp.exp((chunk - 1.0 - rows) * lg)
    cross_dec = jnp.exp((rows + 1.0) * lg)
    chunk_dec = jnp.exp(jnp.full((1, 1), float(chunk), F32) * lg)
    ks = jnp.asarray(k_scale, BF16)

    @pl.when(ci == 0)
    def _():
        ri = lax.broadcasted_iota(jnp.int32, (chunk, chunk), 0)
        cc = lax.broadcasted_iota(jnp.int32, (chunk, chunk), 1)
        diff = (ri - cc).astype(F32)
        dec_s[...] = jnp.where(diff >= 0, jnp.exp(jnp.maximum(diff, 0.0) * lg), 0.0)
        kd = ((mk_ref[...] * ks).astype(F32) * end_dec).astype(BF16)
        state_s[...] = _dot_tn(kd, mv_ref[...])

    state = state_s[...]
    for r0 in range(0, q_ref.shape[0], chunk):
        rs = slice(r0, r0 + chunk)
        qc = q_ref[rs, :]
        kc = k_ref[rs, :] * ks
        vc = v_ref[rs, :]
        a = _dot_nt(qc, kc) * dec_s[...]
        o = _dot(a.astype(BF16), vc) + _dot(qc, state.astype(BF16)) * cross_dec
        kd = (kc.astype(F32) * end_dec).astype(BF16)
        state = state * chunk_dec + _dot_tn(kd, vc)
        g = g_ref[rs, :].astype(F32)
        o_ref[rs, :] = (_rms(o) * (g * jax.nn.sigmoid(g))).astype(o_ref.dtype)
    state_s[...] = state


def _retention(proj, mk, mv, *, batch, seq, chunk, step_rows, d_model):
    n = batch * seq
    nc = seq // step_rows
    dk = d_model // RET_HEADS
    dv = 2 * d_model // RET_HEADS
    q_blk0 = (1 * d_model) // dk
    k_blk0 = (2 * d_model) // dk
    v_blk0 = (3 * d_model) // dv
    g_blk0 = (5 * d_model) // dv
    log_gamma = jnp.asarray(np.log(1.0 - 2.0 ** (-5.0 - np.arange(RET_HEADS, dtype=np.float32))), F32)
    return pl.pallas_call(
        functools.partial(_ret_kernel, chunk=chunk, k_scale=dk ** -0.5),
        grid=(batch, RET_HEADS, nc),
        in_specs=[
            pl.BlockSpec(memory_space=pltpu.SMEM),
            pl.BlockSpec((step_rows, dk), lambda b, h, c: (b * nc + c, q_blk0 + h)),
            pl.BlockSpec((step_rows, dk), lambda b, h, c: (b * nc + c, k_blk0 + h)),
            pl.BlockSpec((step_rows, dv), lambda b, h, c: (b * nc + c, v_blk0 + h)),
            pl.BlockSpec((step_rows, dv), lambda b, h, c: (b * nc + c, g_blk0 + h)),
            pl.BlockSpec((chunk, dk), lambda b, h, c: (0, h)),
            pl.BlockSpec((chunk, dv), lambda b, h, c: (0, h)),
        ],
        out_specs=pl.BlockSpec((step_rows, dv), lambda b, h, c: (b * nc + c, h)),
        out_shape=jax.ShapeDtypeStruct((n, RET_HEADS * dv), BF16),
        scratch_shapes=[pltpu.VMEM((dk, dv), F32), pltpu.VMEM((chunk, chunk), F32)],
        compiler_params=pltpu.CompilerParams(
            dimension_semantics=("parallel", "parallel", "arbitrary"),
            vmem_limit_bytes=V7X_VMEM_LIMIT_BYTES),
    )(log_gamma, proj, proj, proj, proj, mk, mv)


def _merge_kernel(x_ref, yda_ref, yret_ref, ga_ref, gr_ref, wda_ref, wret_ref, wout_ref, o_ref):
    a = _dot(yda_ref[...], wda_ref[...])
    r = _dot(yret_ref[...], wret_ref[...])
    merged = (jax.nn.sigmoid(ga_ref[...].astype(F32)) * a
              + jax.nn.sigmoid(gr_ref[...].astype(F32)) * r)
    o_ref[...] = x_ref[...] + _dot(merged.astype(BF16), wout_ref[...])


def _merge(x2, y_da, y_ret, proj, w_da, w_ret, w_out, *, tm):
    n, d = x2.shape
    ga_blk = 7
    gr_blk = 8
    const = lambda shape: pl.BlockSpec(shape, lambda i: (0, 0), pipeline_mode=pl.Buffered(1))
    return pl.pallas_call(
        _merge_kernel,
        grid=(n // tm,),
        in_specs=[
            pl.BlockSpec((tm, d), lambda i: (i, 0)),
            pl.BlockSpec((tm, y_da.shape[1]), lambda i: (i, 0)),
            pl.BlockSpec((tm, y_ret.shape[1]), lambda i: (i, 0)),
            pl.BlockSpec((tm, d), lambda i: (i, ga_blk)),
            pl.BlockSpec((tm, d), lambda i: (i, gr_blk)),
            const(w_da.shape), const(w_ret.shape), const(w_out.shape),
        ],
        out_specs=pl.BlockSpec((tm, d), lambda i: (i, 0)),
        out_shape=jax.ShapeDtypeStruct((n, d), F32),
        compiler_params=pltpu.CompilerParams(
            dimension_semantics=("parallel",),
            vmem_limit_bytes=V7X_VMEM_LIMIT_BYTES),
    )(x2, y_da, y_ret, proj, proj, w_da, w_ret, w_out)


def _mlp_kernel(h_ref, g1_ref, wup_ref, wdown_ref, g2_ref, o_ref, *, tf):
    hres = h_ref[...]
    u = (_rms(hres) * g1_ref[...]).astype(BF16)
    acc = hres
    for f in range(0, wup_ref.shape[1], tf):
        a = jnp.maximum(_dot(u, wup_ref[:, f:f + tf]), 0.0)
        acc = acc + _dot((a * a).astype(BF16), wdown_ref[f:f + tf, :])
    o_ref[...] = _rms(acc) * g2_ref[...]


def _mlp(h1, g1, w_up, w_down, g2, *, tm, tf):
    n, d = h1.shape
    const = lambda shape: pl.BlockSpec(shape, lambda i: (0, 0), pipeline_mode=pl.Buffered(1))
    return pl.pallas_call(
        functools.partial(_mlp_kernel, tf=tf),
        grid=(n // tm,),
        in_specs=[
            pl.BlockSpec((tm, d), lambda i: (i, 0)),
            pl.BlockSpec((1, d), lambda i: (0, 0)),
            const(w_up.shape), const(w_down.shape),
            pl.BlockSpec((1, d), lambda i: (0, 0)),
        ],
        out_specs=pl.BlockSpec((tm, d), lambda i: (i, 0)),
        out_shape=jax.ShapeDtypeStruct((n, d), F32),
        compiler_params=pltpu.CompilerParams(
            dimension_semantics=("parallel",),
            vmem_limit_bytes=V7X_VMEM_LIMIT_BYTES),
    )(h1, g1, w_up, w_down, g2)


def _tile(n, target):
    t = min(n, target)
    assert n % t == 0, (n, t)
    return t


def kernel(x, meta_tokens, norm_mix_g, w_in, da_lambda_q1, da_lambda_k1, da_lambda_q2, da_lambda_k2,
           da_head_g, w_proj_da, w_proj_ret, w_out, norm_mlp_g, w_up, w_down, final_norm_g):
    batch, seq, d = x.shape
    assert w_in.shape[0] == 1, "single-layer block"
    assert d // RET_HEADS == 2 * LANES and DA_HEADS * DA_VDIM == d
    n = batch * seq
    layer = 0
    lam_init = 0.8 - 0.6 * math.exp(-0.3 * layer)

    x2 = x.reshape(n, d)
    w_in_b = w_in[layer].astype(BF16)
    g_mix = norm_mix_g[layer].reshape(1, d)

    w_rows = jnp.concatenate([w_in_b[:, d:2 * d], w_in_b[:, 3 * d:]], axis=1)
    w_feat = jnp.concatenate([w_in_b[:, :d], w_in_b[:, 2 * d:3 * d]], axis=1).T
    proj, proj_t = _inproj_t(x2, g_mix, w_rows, w_feat, tm=_tile(n, 1024), tn=_tile(w_rows.shape[1], 2304))
    proj_m = _inproj(meta_tokens.astype(x.dtype), g_mix, w_in_b, tm=N_META, tn=_tile(w_in_b.shape[1], 1024))

    mvt_a = jnp.pad(proj_m[:, 2 * d:3 * d].T, ((0, 0), (0, LANES - N_META)))
    row = lambda v: v[layer].reshape(1, -1).astype(F32)
    y_da = _diff_attention(proj, proj_t, proj_m[:, d:2 * d], mvt_a, row(da_lambda_q1), row(da_lambda_k1),
                           row(da_lambda_q2), row(da_lambda_k2), row(da_head_g), batch=batch, seq=seq,
                           lam_init=lam_init)

    chunk = _tile(seq, 256)
    pad_r = ((chunk - N_META, 0), (0, 0))
    mk_r = jnp.pad(proj_m[:, 4 * d:5 * d], pad_r)
    mv_r = jnp.pad(proj_m[:, 5 * d:7 * d], pad_r)
    y_ret = _retention(proj, mk_r, mv_r, batch=batch, seq=seq, chunk=chunk, step_rows=_tile(seq, 1024),
                       d_model=d)

    h1 = _merge(x2, y_da, y_ret, proj, w_proj_da[layer].astype(BF16), w_proj_ret[layer].astype(BF16),
                w_out[layer].astype(BF16), tm=_tile(n, 512))
    out = _mlp(h1, norm_mlp_g[layer].reshape(1, d), w_up[layer].astype(BF16), w_down[layer].astype(BF16),
               final_norm_g.reshape(1, d), tm=_tile(n, 512), tf=1024)
    return out.reshape(batch, seq, d)
```

```python
import functools
import math

import numpy as np
import jax
import jax.numpy as jnp
from jax import lax
from jax.experimental import pallas as pl
from jax.experimental.pallas import tpu as pltpu

N_META = 16
DA_HEADS = 8
DA_HEAD_DIM = 64
DA_VDIM = 2 * DA_HEAD_DIM
RET_HEADS = 4
NORM_EPS = 1e-6

F32 = jnp.float32
BF16 = jnp.bfloat16

V7X_VMEM_LIMIT_BYTES = 56 * 1024 * 1024
LANES = 128
BF16_SUBLANES = 16


def _dot(a, b):
    return jnp.dot(a, b, preferred_element_type=F32)


def _dot_nt(a, b):
    return lax.dot_general(a, b, (((1,), (1,)), ((), ())), preferred_element_type=F32)


def _dot_tn(a, b):
    return lax.dot_general(a, b, (((0,), (0,)), ((), ())), preferred_element_type=F32)


def _rms(xf):
    return xf * lax.rsqrt(jnp.mean(xf * xf, axis=-1, keepdims=True) + NORM_EPS)


PROJ_COLS = {"k_a": 0, "q_r": 1, "k_r": 2, "v_r": 3, "gate_r": 5, "g_a": 7, "g_r": 8}


def _inproj_kernel(x_ref, g_ref, w_ref, o_ref, u_ref):
    @pl.when(pl.program_id(1) == 0)
    def _():
        u_ref[...] = (_rms(x_ref[...]) * g_ref[...]).astype(BF16)

    o_ref[...] = _dot(u_ref[...], w_ref[...]).astype(o_ref.dtype)


def _inproj_t_kernel(x_ref, g_ref, w_ref, wt_ref, o_ref, ot_ref, u_ref):
    @pl.when(pl.program_id(1) == 0)
    def _():
        u = (_rms(x_ref[...]) * g_ref[...]).astype(BF16)
        u_ref[...] = u
        ot_ref[...] = _dot_nt(wt_ref[...], u).astype(ot_ref.dtype)

    o_ref[...] = _dot(u_ref[...], w_ref[...]).astype(o_ref.dtype)


def _inproj(x2, g, w, *, tm, tn):
    n, d = x2.shape
    d_in = w.shape[1]
    return pl.pallas_call(
        _inproj_kernel,
        grid=(n // tm, d_in // tn),
        in_specs=[
            pl.BlockSpec((tm, d), lambda i, j: (i, 0)),
            pl.BlockSpec((1, d), lambda i, j: (0, 0)),
            pl.BlockSpec((d, tn), lambda i, j: (0, j)),
        ],
        out_specs=pl.BlockSpec((tm, tn), lambda i, j: (i, j)),
        out_shape=jax.ShapeDtypeStruct((n, d_in), BF16),
        scratch_shapes=[pltpu.VMEM((tm, d), BF16)],
        compiler_params=pltpu.CompilerParams(
            dimension_semantics=("parallel", "arbitrary"),
            vmem_limit_bytes=V7X_VMEM_LIMIT_BYTES),
    )(x2, g, w)


def _inproj_t(x2, g, w, wt, *, tm, tn):
    n, d = x2.shape
    d_in = w.shape[1]
    d_t = wt.shape[0]
    return pl.pallas_call(
        _inproj_t_kernel,
        grid=(n // tm, d_in // tn),
        in_specs=[
            pl.BlockSpec((tm, d), lambda i, j: (i, 0)),
            pl.BlockSpec((1, d), lambda i, j: (0, 0)),
            pl.BlockSpec((d, tn), lambda i, j: (0, j)),
            pl.BlockSpec((d_t, d), lambda i, j: (0, 0), pipeline_mode=pl.Buffered(1)),
        ],
        out_specs=[pl.BlockSpec((tm, tn), lambda i, j: (i, j)),
                   pl.BlockSpec((d_t, tm), lambda i, j: (0, i))],
        out_shape=[jax.ShapeDtypeStruct((n, d_in), BF16), jax.ShapeDtypeStruct((d_t, n), BF16)],
        scratch_shapes=[pltpu.VMEM((tm, d), BF16)],
        compiler_params=pltpu.CompilerParams(
            dimension_semantics=("parallel", "arbitrary"),
            vmem_limit_bytes=V7X_VMEM_LIMIT_BYTES),
    )(x2, g, w, wt)


ATTN_TQ = 1024
ATTN_SUB = 256
ATTN_CHUNK = 1024
ATTN_VT_COLS = 512
N_FEAT = 4
LOG2_E = math.log2(math.e)
SKIP_BELOW = 140.0


def _attn_kernel(slopes_ref, qt_ref, k_ref, vt_ref, mk_ref, mvt_ref, lq1_ref, lk1_ref, lq2_ref, lk2_ref,
                 hg_ref, o_ref, mvt_s, vt_s, kpos_s, kabs_s, qt_s, s_s, sm_s, m_s, acc_s, *, lam_init):
    h = pl.program_id(1)
    i = pl.program_id(2)
    slope = slopes_ref[h]
    inv_slope = slopes_ref[DA_HEADS + h]
    tq, sub, chunk, vtc = ATTN_TQ, ATTN_SUB, ATTN_CHUNK, ATTN_VT_COLS
    n_sub = tq // sub
    dv = vt_ref.shape[0]
    n_cols = 2 * tq
    neg_inf = jnp.float32(-jnp.inf)
    lane_row = lax.broadcasted_iota(jnp.int32, (1, LANES), 1)

    @pl.when(i == 0)
    def _():
        n_ones = vt_s.shape[1] - dv
        for c in range(vt_s.shape[0]):
            vt_s[c, :dv, :] = vt_ref[:, c * vtc:(c + 1) * vtc]
            vt_s[c, dv:, :] = jnp.ones((n_ones, vtc), vt_s.dtype)
        mvt_s[:dv, :] = mvt_ref[...]
        mvt_s[dv:, :] = jnp.ones((n_ones, LANES), mvt_s.dtype)
        pos = lax.broadcasted_iota(jnp.int32, kpos_s.shape, 0).astype(F32) * slope
        plane = lax.broadcasted_iota(jnp.int32, kpos_s.shape, 1)
        pos_hi = pos.astype(BF16).astype(F32)
        kpos_s[...] = jnp.where(plane == 0, pos_hi, jnp.where(plane == 1, pos - pos_hi, 0.0))
        frow = lax.broadcasted_iota(jnp.int32, (LANES, n_cols), 0)
        qt_s[LANES:, :] = jnp.where(frow < N_FEAT, 1.0, 0.0).astype(qt_s.dtype)
        kabs = jnp.max(jnp.abs(k_ref[...].astype(F32)), axis=0, keepdims=True)
        kabs_s[...] = jnp.broadcast_to(kabs, kabs_s.shape).T

    frow = lax.broadcasted_iota(jnp.int32, (LANES, sub), 0)
    for c in range(n_sub):
        qf = qt_ref[:, c * sub:(c + 1) * sub].astype(F32) * (DA_HEAD_DIM ** -0.5 * LOG2_E)
        qt_s[:LANES, 2 * c * sub:(2 * c + 1) * sub] = jnp.where(frow < DA_HEAD_DIM, qf, 0.0).astype(qt_s.dtype)
        qt_s[:LANES, (2 * c + 1) * sub:(2 * c + 2) * sub] = (
            jnp.where(frow >= DA_HEAD_DIM, qf, 0.0).astype(qt_s.dtype))

    def key_feats(row0, n_rows, offset):
        off = jnp.full((1, LANES), offset, F32)
        off_hi = off.astype(BF16).astype(F32)
        off_row = jnp.where(lane_row == 2, off_hi, jnp.where(lane_row == 3, off - off_hi, 0.0))
        return (kpos_s[row0:row0 + n_rows, :] + off_row).astype(BF16)

    def with_feats(k_blk, row0, offset):
        return jnp.concatenate([k_blk, key_feats(row0, k_blk.shape[0], offset)], axis=1)

    diag = pl.multiple_of(i * tq, tq)
    kr = lax.broadcasted_iota(jnp.int32, (sub, 2 * sub), 0)
    qc = lax.broadcasted_iota(jnp.int32, (sub, 2 * sub), 1)
    causal = kr <= (qc & (sub - 1))
    for r in range(n_sub):
        c0 = 2 * sub * r
        lhs = with_feats(k_ref[pl.ds(diag + r * sub, sub), :], r * sub, 0.0)
        if r == 0:
            lhs = jnp.concatenate(
                [lhs, with_feats(mk_ref[...], 0, -(i * tq + N_META).astype(F32) * slope)], axis=0)
        s = _dot(lhs, qt_s[:, c0:])
        s_first = jnp.where(causal, s[:sub, :2 * sub], neg_inf)
        s_top = s_first if r == n_sub - 1 else jnp.concatenate([s_first, s[:sub, 2 * sub:]], axis=1)
        s_s[pl.ds(diag + r * sub, sub), c0:] = s_top
        m_r = jnp.max(s_top, axis=0, keepdims=True)
        if r == 0:
            sm_s[...] = s[sub:]
            m_s[...] = jnp.maximum(m_r, jnp.max(s[sub:], axis=0, keepdims=True))
        else:
            m_s[:, c0:] = jnp.maximum(m_s[:, c0:], m_r)

    def score_body(t, carry):
        start = pl.multiple_of(t * chunk, chunk)
        s = _dot(with_feats(k_ref[pl.ds(start, chunk), :], 0, (start - i * tq).astype(F32) * slope),
                 qt_s[...])
        s_s[pl.ds(start, chunk), :] = s
        m_s[...] = jnp.maximum(m_s[...], jnp.max(s, axis=0, keepdims=True))
        return carry

    q_abs = jnp.abs(qt_s[:LANES, :].astype(F32))
    q_bound = jnp.sum(q_abs * jnp.tile(kabs_s[...], (1, n_cols // LANES)), axis=0, keepdims=True)
    dist = (SKIP_BELOW + jnp.max(q_bound - m_s[...])) * inv_slope
    n_chunks = i * (tq // chunk)
    n_skip = jnp.floor(i.astype(F32) * (tq / chunk) - (dist - 1.0) / chunk)
    n_skip = jnp.clip(n_skip.astype(jnp.int32), 0, n_chunks)
    lax.fori_loop(n_skip, n_chunks, score_body, 0)

    m = m_s[...]
    for r in range(n_sub):
        c0 = 2 * sub * r
        vt_r = vt_s[(diag + r * sub) // vtc][:, (r * sub) % vtc:(r * sub) % vtc + sub]
        p = jnp.exp2((s_s[pl.ds(diag + r * sub, sub), c0:] - m[:, c0:]).astype(BF16))
        if r == 0:
            p = jnp.concatenate([p, jnp.exp2((sm_s[...] - m).astype(BF16)),
                                 jnp.zeros((LANES - N_META, n_cols), BF16)], axis=0)
            acc_s[...] = _dot(jnp.concatenate([vt_r, mvt_s[...]], axis=1), p)
        else:
            acc_s[:, c0:] += _dot(vt_r, p)

    def pv_body(t, carry):
        start = pl.multiple_of(t * chunk, chunk)
        p = jnp.exp2((s_s[pl.ds(start, chunk), :] - m).astype(BF16))
        blk = start // vtc
        upd = _dot(vt_s[blk], p[:vtc])
        for c in range(1, chunk // vtc):
            upd = upd + _dot(vt_s[blk + c], p[c * vtc:(c + 1) * vtc])
        acc_s[...] += upd
        return carry

    lax.fori_loop(n_skip, n_chunks, pv_body, 0)

    lam = (jnp.exp(jnp.sum(lq1_ref[...] * lk1_ref[...], axis=1, keepdims=True))
           - jnp.exp(jnp.sum(lq2_ref[...] * lk2_ref[...], axis=1, keepdims=True)) + lam_init)
    inv_l = 1.0 / acc_s[dv:dv + 1, :]
    for c in range(n_sub):
        c0 = 2 * sub * c
        o1 = acc_s[:dv, c0:c0 + sub] * inv_l[:, c0:c0 + sub]
        o2 = acc_s[:dv, c0 + sub:c0 + 2 * sub] * inv_l[:, c0 + sub:c0 + 2 * sub]
        ot = o1 - lam * o2
        yt = ot * lax.rsqrt(jnp.mean(ot * ot, axis=0, keepdims=True) + NORM_EPS)
        o_ref[c * sub:(c + 1) * sub, :] = (yt.T * hg_ref[...] * (1.0 - lam_init)).astype(o_ref.dtype)


def _diff_attention(proj, proj_t, mk, mvt, lq1, lk1, lq2, lk2, head_g, *, batch, seq, lam_init):
    n = batch * seq
    tq = ATTN_TQ
    assert seq % tq == 0 and tq % ATTN_CHUNK == 0 and ATTN_CHUNK % ATTN_VT_COLS == 0
    nq = seq // tq
    dv = DA_VDIM
    slopes = 2.0 ** (-8.0 * (np.arange(DA_HEADS, dtype=np.float64) + 1.0) / DA_HEADS) * LOG2_E
    slopes = jnp.asarray(np.concatenate([slopes, 1.0 / slopes]), F32)
    vec = lambda: pl.BlockSpec((1, DA_HEAD_DIM), lambda b, h, i: (0, 0))
    n_cols = 2 * tq
    n_vrows = dv + BF16_SUBLANES
    return pl.pallas_call(
        functools.partial(_attn_kernel, lam_init=lam_init),
        grid=(batch, DA_HEADS, nq),
        in_specs=[
            pl.BlockSpec(memory_space=pltpu.SMEM),
            pl.BlockSpec((dv, tq), lambda b, h, i: (h, b * nq + i)),
            pl.BlockSpec((seq, dv), lambda b, h, i: (b, PROJ_COLS["k_a"] * DA_HEADS + h)),
            pl.BlockSpec((dv, seq), lambda b, h, i: (DA_HEADS + h, b)),
            pl.BlockSpec((N_META, dv), lambda b, h, i: (0, h)),
            pl.BlockSpec((dv, LANES), lambda b, h, i: (h, 0)),
            vec(), vec(), vec(), vec(),
            pl.BlockSpec((1, dv), lambda b, h, i: (0, 0)),
        ],
        out_specs=pl.BlockSpec((tq, dv), lambda b, h, i: (b * nq + i, h)),
        out_shape=jax.ShapeDtypeStruct((n, DA_HEADS * dv), BF16),
        scratch_shapes=[pltpu.VMEM((n_vrows, LANES), BF16),
                        pltpu.VMEM((seq // ATTN_VT_COLS, n_vrows, ATTN_VT_COLS), BF16),
                        pltpu.VMEM((max(tq, ATTN_CHUNK), LANES), F32),
                        pltpu.VMEM((LANES, LANES), F32),
                        pltpu.VMEM((2 * LANES, n_cols), BF16),
                        pltpu.VMEM((seq, n_cols), F32),
                        pltpu.VMEM((N_META, n_cols), F32),
                        pltpu.VMEM((1, n_cols), F32),
                        pltpu.VMEM((n_vrows, n_cols), F32)],
        compiler_params=pltpu.CompilerParams(
            dimension_semantics=("parallel", "parallel", "arbitrary"),
            vmem_limit_bytes=V7X_VMEM_LIMIT_BYTES),
    )(slopes, proj_t, proj, proj_t, mk, mvt, lq1, lk1, lq2, lk2, head_g)


def _ret_kernel(lg_ref, q_ref, k_ref, v_ref, mk_ref, mv_ref, o_ref, state_s, dec_s, *, chunk, k_scale):
    h = pl.program_id(1)
    ci = pl.program_id(2)
    lg = lg_ref[h]
    rows = lax.broadcasted_iota(jnp.int32, (chunk, 1), 0).astype(F32)
    end_dec = jnp.exp((chunk - 1.0 - rows) * lg)
    cross_dec = jnp.exp((rows + 1.0) * lg)
    chunk_dec = jnp.exp(jnp.full((1, 1), float(chunk), F32) * lg)
    ks = jnp.asarray(k_scale, BF16)

    @pl.when(ci == 0)
    def _():
        ri = lax.broadcasted_iota(jnp.int32, (chunk, chunk), 0)
        cc = lax.broadcasted_iota(jnp.int32, (chunk, chunk), 1)
        diff = (ri - cc).astype(F32)
        dec_s[...] = jnp.where(diff >= 0, jnp.exp(jnp.maximum(diff, 0.0) * lg), 0.0)
        kd = ((mk_ref[...] * ks).astype(F32) * end_dec).astype(BF16)
        state_s[...] = _dot_tn(kd, mv_ref[...])

    state = state_s[...]
    for r0 in range(0, q_ref.shape[0], chunk):
        rs = slice(r0, r0 + chunk)
        qc = q_ref[rs, :]
        kc = k_ref[rs, :] * ks
        vc = v_ref[rs, :]
        a = _dot_nt(qc, kc) * dec_s[...]
        o = _dot(a.astype(BF16), vc) + _dot(qc, state.astype(BF16)) * cross_dec
        kd = (kc.astype(F32) * end_dec).astype(BF16)
        state = state * chunk_dec + _dot_tn(kd, vc)
        o_ref[rs, :] = o.astype(o_ref.dtype)
    state_s[...] = state


def _retention(proj, mk, mv, *, batch, seq, chunk, step_rows, d_model):
    n = batch * seq
    nc = seq // step_rows
    dk = d_model // RET_HEADS
    dv = 2 * d_model // RET_HEADS
    q_blk0 = (PROJ_COLS["q_r"] * d_model) // dk
    k_blk0 = (PROJ_COLS["k_r"] * d_model) // dk
    v_blk0 = (PROJ_COLS["v_r"] * d_model) // dv
    log_gamma = jnp.asarray(np.log(1.0 - 2.0 ** (-5.0 - np.arange(RET_HEADS, dtype=np.float32))), F32)
    return pl.pallas_call(
        functools.partial(_ret_kernel, chunk=chunk, k_scale=dk ** -0.5),
        grid=(batch, RET_HEADS, nc),
        in_specs=[
            pl.BlockSpec(memory_space=pltpu.SMEM),
            pl.BlockSpec((step_rows, dk), lambda b, h, c: (b * nc + c, q_blk0 + h)),
            pl.BlockSpec((step_rows, dk), lambda b, h, c: (b * nc + c, k_blk0 + h)),
            pl.BlockSpec((step_rows, dv), lambda b, h, c: (b * nc + c, v_blk0 + h)),
            pl.BlockSpec((chunk, dk), lambda b, h, c: (0, h)),
            pl.BlockSpec((chunk, dv), lambda b, h, c: (0, h)),
        ],
        out_specs=pl.BlockSpec((step_rows, dv), lambda b, h, c: (b * nc + c, h)),
        out_shape=jax.ShapeDtypeStruct((n, RET_HEADS * dv), BF16),
        scratch_shapes=[pltpu.VMEM((dk, dv), F32), pltpu.VMEM((chunk, chunk), F32)],
        compiler_params=pltpu.CompilerParams(
            dimension_semantics=("parallel", "parallel", "arbitrary"),
            vmem_limit_bytes=V7X_VMEM_LIMIT_BYTES),
    )(log_gamma, proj, proj, proj, mk, mv)


def _merge_kernel(x_ref, yda_ref, oret_ref, gate0_ref, gate1_ref, ga_ref, gr_ref, wda_ref, wret_ref, wout_ref,
                  o_ref):
    a = _dot(yda_ref[...], wda_ref[...])
    dv = oret_ref.shape[1] // RET_HEADS
    gates = jnp.concatenate([gate0_ref[...], gate1_ref[...]], axis=1).astype(F32)
    heads = []
    for hh in range(RET_HEADS):
        g = gates[:, hh * dv:(hh + 1) * dv]
        heads.append((_rms(oret_ref[:, hh * dv:(hh + 1) * dv].astype(F32)) * (g * jax.nn.sigmoid(g))).astype(BF16))
    r = _dot(jnp.concatenate(heads, axis=1), wret_ref[...])
    merged = (jax.nn.sigmoid(ga_ref[...].astype(F32)) * a
              + jax.nn.sigmoid(gr_ref[...].astype(F32)) * r)
    o_ref[...] = x_ref[...] + _dot(merged.astype(BF16), wout_ref[...])


def _merge(x2, y_da, o_ret, proj, w_da, w_ret, w_out, *, tm):
    n, d = x2.shape
    col = lambda name, off=0: pl.BlockSpec((tm, d), lambda i: (i, PROJ_COLS[name] + off))
    const = lambda shape: pl.BlockSpec(shape, lambda i: (0, 0), pipeline_mode=pl.Buffered(1))
    return pl.pallas_call(
        _merge_kernel,
        grid=(n // tm,),
        in_specs=[
            pl.BlockSpec((tm, d), lambda i: (i, 0)),
            pl.BlockSpec((tm, y_da.shape[1]), lambda i: (i, 0)),
            pl.BlockSpec((tm, o_ret.shape[1]), lambda i: (i, 0)),
            col("gate_r"), col("gate_r", 1), col("g_a"), col("g_r"),
            const(w_da.shape), const(w_ret.shape), const(w_out.shape),
        ],
        out_specs=pl.BlockSpec((tm, d), lambda i: (i, 0)),
        out_shape=jax.ShapeDtypeStruct((n, d), F32),
        compiler_params=pltpu.CompilerParams(
            dimension_semantics=("parallel",),
            vmem_limit_bytes=V7X_VMEM_LIMIT_BYTES),
    )(x2, y_da, o_ret, proj, proj, proj, proj, w_da, w_ret, w_out)


def _mlp_kernel(h_ref, g1_ref, wup_ref, wdown_ref, g2_ref, o_ref, *, tf):
    hres = h_ref[...]
    u = (_rms(hres) * g1_ref[...]).astype(BF16)
    acc = hres
    for f in range(0, wup_ref.shape[1], tf):
        a = jnp.maximum(_dot(u, wup_ref[:, f:f + tf]), 0.0)
        acc = acc + _dot((a * a).astype(BF16), wdown_ref[f:f + tf, :])
    o_ref[...] = _rms(acc) * g2_ref[...]


def _mlp(h1, g1, w_up, w_down, g2, *, tm, tf):
    n, d = h1.shape
    const = lambda shape: pl.BlockSpec(shape, lambda i: (0, 0), pipeline_mode=pl.Buffered(1))
    return pl.pallas_call(
        functools.partial(_mlp_kernel, tf=tf),
        grid=(n // tm,),
        in_specs=[
            pl.BlockSpec((tm, d), lambda i: (i, 0)),
            pl.BlockSpec((1, d), lambda i: (0, 0)),
            const(w_up.shape), const(w_down.shape),
            pl.BlockSpec((1, d), lambda i: (0, 0)),
        ],
        out_specs=pl.BlockSpec((tm, d), lambda i: (i, 0)),
        out_shape=jax.ShapeDtypeStruct((n, d), F32),
        compiler_params=pltpu.CompilerParams(
            dimension_semantics=("parallel",),
            vmem_limit_bytes=V7X_VMEM_LIMIT_BYTES),
    )(h1, g1, w_up, w_down, g2)


def _tile(n, target):
    t = min(n, target)
    assert n % t == 0, (n, t)
    return t


def kernel(x, meta_tokens, norm_mix_g, w_in, da_lambda_q1, da_lambda_k1, da_lambda_q2, da_lambda_k2,
           da_head_g, w_proj_da, w_proj_ret, w_out, norm_mlp_g, w_up, w_down, final_norm_g):
    batch, seq, d = x.shape
    assert w_in.shape[0] == 1, "single-layer block"
    assert d // RET_HEADS == 2 * LANES and DA_HEADS * DA_VDIM == d
    n = batch * seq
    layer = 0
    lam_init = 0.8 - 0.6 * math.exp(-0.3 * layer)

    x2 = x.reshape(n, d)
    w_in_b = w_in[layer].astype(BF16)
    g_mix = norm_mix_g[layer].reshape(1, d)

    w_rows = jnp.concatenate([w_in_b[:, d:2 * d], w_in_b[:, 3 * d:]], axis=1)
    w_feat = jnp.concatenate([w_in_b[:, :d], w_in_b[:, 2 * d:3 * d]], axis=1).T
    proj, proj_t = _inproj_t(x2, g_mix, w_rows, w_feat, tm=_tile(n, 1024), tn=_tile(w_rows.shape[1], 2304))
    proj_m = _inproj(meta_tokens.astype(x.dtype), g_mix, w_in_b, tm=N_META, tn=_tile(w_in_b.shape[1], 1024))

    mvt_a = jnp.pad(proj_m[:, 2 * d:3 * d].T, ((0, 0), (0, LANES - N_META)))
    row = lambda v: v[layer].reshape(1, -1).astype(F32)
    y_da = _diff_attention(proj, proj_t, proj_m[:, d:2 * d], mvt_a, row(da_lambda_q1), row(da_lambda_k1),
                           row(da_lambda_q2), row(da_lambda_k2), row(da_head_g), batch=batch, seq=seq,
                           lam_init=lam_init)

    chunk = _tile(seq, 256)
    pad_r = ((chunk - N_META, 0), (0, 0))
    mk_r = jnp.pad(proj_m[:, 4 * d:5 * d], pad_r)
    mv_r = jnp.pad(proj_m[:, 5 * d:7 * d], pad_r)
    o_ret = _retention(proj, mk_r, mv_r, batch=batch, seq=seq, chunk=chunk, step_rows=_tile(seq, 1024),
                       d_model=d)

    h1 = _merge(x2, y_da, o_ret, proj, w_proj_da[layer].astype(BF16), w_proj_ret[layer].astype(BF16),
                w_out[layer].astype(BF16), tm=_tile(n, 512))
    out = _mlp(h1, norm_mlp_g[layer].reshape(1, d), w_up[layer].astype(BF16), w_down[layer].astype(BF16),
               final_norm_g.reshape(1, d), tm=_tile(n, 512), tf=1024)
    return out.reshape(batch, seq, d)
```

```python
import functools
import math

import numpy as np
import jax
import jax.numpy as jnp
from jax import lax
from jax.experimental import pallas as pl
from jax.experimental.pallas import tpu as pltpu

N_META = 16
DA_HEADS = 8
DA_HEAD_DIM = 64
DA_VDIM = 2 * DA_HEAD_DIM
RET_HEADS = 4
NORM_EPS = 1e-6

F32 = jnp.float32
BF16 = jnp.bfloat16

V7X_VMEM_LIMIT_BYTES = 56 * 1024 * 1024
LANES = 128
BF16_SUBLANES = 16


def _dot(a, b):
    return jnp.dot(a, b, preferred_element_type=F32)


def _dot_nt(a, b):
    return lax.dot_general(a, b, (((1,), (1,)), ((), ())), preferred_element_type=F32)


def _dot_tn(a, b):
    return lax.dot_general(a, b, (((0,), (0,)), ((), ())), preferred_element_type=F32)


def _rms(xf):
    return xf * lax.rsqrt(jnp.mean(xf * xf, axis=-1, keepdims=True) + NORM_EPS)


PROJ_COLS = {"k_a": 0, "q_r": 1, "k_r": 2, "v_r": 3, "gate_r": 5, "g_a": 7, "g_r": 8}


def _inproj_kernel(x_ref, g_ref, w_ref, o_ref, u_ref):
    @pl.when(pl.program_id(1) == 0)
    def _():
        u_ref[...] = (_rms(x_ref[...]) * g_ref[...]).astype(BF16)

    o_ref[...] = _dot(u_ref[...], w_ref[...]).astype(o_ref.dtype)


def _inproj_t_kernel(x_ref, g_ref, w_ref, wt_ref, o_ref, ot_ref, *, n_scaled, scale, col_group):
    u = (_rms(x_ref[...]) * g_ref[...]).astype(BF16)
    ot = _dot_nt(wt_ref[...], u)
    ot_ref[:n_scaled, :] = (ot[:n_scaled] * scale).astype(ot_ref.dtype)
    ot_ref[n_scaled:, :] = ot[n_scaled:].astype(ot_ref.dtype)
    for c0 in range(0, w_ref.shape[1], col_group):
        o_ref[:, c0:c0 + col_group] = _dot(u, w_ref[:, c0:c0 + col_group]).astype(o_ref.dtype)


def _inproj(x2, g, w, *, tm, tn):
    n, d = x2.shape
    d_in = w.shape[1]
    return pl.pallas_call(
        _inproj_kernel,
        grid=(n // tm, d_in // tn),
        in_specs=[
            pl.BlockSpec((tm, d), lambda i, j: (i, 0)),
            pl.BlockSpec((1, d), lambda i, j: (0, 0)),
            pl.BlockSpec((d, tn), lambda i, j: (0, j)),
        ],
        out_specs=pl.BlockSpec((tm, tn), lambda i, j: (i, j)),
        out_shape=jax.ShapeDtypeStruct((n, d_in), BF16),
        scratch_shapes=[pltpu.VMEM((tm, d), BF16)],
        compiler_params=pltpu.CompilerParams(
            dimension_semantics=("parallel", "arbitrary"),
            vmem_limit_bytes=V7X_VMEM_LIMIT_BYTES),
    )(x2, g, w)


def _inproj_t(x2, g, w, wt, *, tm, col_group, n_scaled, scale):
    n, d = x2.shape
    d_in = w.shape[1]
    d_t = wt.shape[0]
    const = lambda shape: pl.BlockSpec(shape, lambda i: (0, 0), pipeline_mode=pl.Buffered(1))
    return pl.pallas_call(
        functools.partial(_inproj_t_kernel, n_scaled=n_scaled, scale=scale, col_group=col_group),
        grid=(n // tm,),
        in_specs=[
            pl.BlockSpec((tm, d), lambda i: (i, 0)),
            pl.BlockSpec((1, d), lambda i: (0, 0)),
            const(w.shape), const(wt.shape),
        ],
        out_specs=[pl.BlockSpec((tm, d_in), lambda i: (i, 0)),
                   pl.BlockSpec((d_t, tm), lambda i: (0, i))],
        out_shape=[jax.ShapeDtypeStruct((n, d_in), BF16), jax.ShapeDtypeStruct((d_t, n), BF16)],
        compiler_params=pltpu.CompilerParams(
            dimension_semantics=("parallel",),
            vmem_limit_bytes=V7X_VMEM_LIMIT_BYTES),
    )(x2, g, w, wt)


ATTN_TQ = 1024
ATTN_SUB = 256
ATTN_CHUNK = 1024
ATTN_VT_COLS = 512
N_FEAT = 4
LOG2_E = math.log2(math.e)
SKIP_BELOW = 140.0


def _attn_kernel(slopes_ref, qt_ref, k_ref, vt_ref, mk_ref, mvt_ref, lq1_ref, lk1_ref, lq2_ref, lk2_ref,
                 hg_ref, o_ref, mvt_s, vt_s, kpos_s, kabs_s, qt_s, s_s, sm_s, m_s, acc_s, *, lam_init):
    h = pl.program_id(1)
    i = pl.program_id(2)
    slope = slopes_ref[h]
    inv_slope = slopes_ref[DA_HEADS + h]
    tq, sub, chunk, vtc = ATTN_TQ, ATTN_SUB, ATTN_CHUNK, ATTN_VT_COLS
    n_sub = tq // sub
    dv = vt_ref.shape[0]
    n_cols = 2 * tq
    neg_inf = jnp.float32(-jnp.inf)
    lane_row = lax.broadcasted_iota(jnp.int32, (1, LANES), 1)

    @pl.when(i == 0)
    def _():
        n_ones = vt_s.shape[1] - dv
        for c in range(vt_s.shape[0]):
            vt_s[c, :dv, :] = vt_ref[:, c * vtc:(c + 1) * vtc]
            vt_s[c, dv:, :] = jnp.ones((n_ones, vtc), vt_s.dtype)
        mvt_s[:dv, :] = mvt_ref[...]
        mvt_s[dv:, :] = jnp.ones((n_ones, LANES), mvt_s.dtype)
        pos = lax.broadcasted_iota(jnp.int32, kpos_s.shape, 0).astype(F32) * slope
        plane = lax.broadcasted_iota(jnp.int32, kpos_s.shape, 1)
        pos_hi = pos.astype(BF16).astype(F32)
        kpos_s[...] = jnp.where(plane == 0, pos_hi, jnp.where(plane == 1, pos - pos_hi, 0.0))
        frow = lax.broadcasted_iota(jnp.int32, (LANES, n_cols), 0)
        qt_s[:LANES, :] = jnp.zeros((LANES, n_cols), qt_s.dtype)
        qt_s[LANES:, :] = jnp.where(frow < N_FEAT, 1.0, 0.0).astype(qt_s.dtype)
        kabs = jnp.max(jnp.abs(k_ref[...].astype(F32)), axis=0, keepdims=True)
        kabs_s[...] = jnp.broadcast_to(kabs, kabs_s.shape).T

    for c in range(n_sub):
        qt_s[:DA_HEAD_DIM, 2 * c * sub:(2 * c + 1) * sub] = qt_ref[:DA_HEAD_DIM, c * sub:(c + 1) * sub]
        qt_s[DA_HEAD_DIM:LANES, (2 * c + 1) * sub:(2 * c + 2) * sub] = qt_ref[DA_HEAD_DIM:, c * sub:(c + 1) * sub]

    def key_feats(row0, n_rows, offset):
        off = jnp.full((1, LANES), offset, F32)
        off_hi = off.astype(BF16).astype(F32)
        off_row = jnp.where(lane_row == 2, off_hi, jnp.where(lane_row == 3, off - off_hi, 0.0))
        return (kpos_s[row0:row0 + n_rows, :] + off_row).astype(BF16)

    def with_feats(k_blk, row0, offset):
        return jnp.concatenate([k_blk, key_feats(row0, k_blk.shape[0], offset)], axis=1)

    diag = pl.multiple_of(i * tq, tq)
    kr = lax.broadcasted_iota(jnp.int32, (sub, 2 * sub), 0)
    qc = lax.broadcasted_iota(jnp.int32, (sub, 2 * sub), 1)
    causal = kr <= (qc & (sub - 1))
    for r in range(n_sub):
        c0 = 2 * sub * r
        lhs = with_feats(k_ref[pl.ds(diag + r * sub, sub), :], r * sub, 0.0)
        if r == 0:
            lhs = jnp.concatenate(
                [lhs, with_feats(mk_ref[...], 0, -(i * tq + N_META).astype(F32) * slope)], axis=0)
        s = _dot(lhs, qt_s[:, c0:])
        s_first = jnp.where(causal, s[:sub, :2 * sub], neg_inf)
        s_top = s_first if r == n_sub - 1 else jnp.concatenate([s_first, s[:sub, 2 * sub:]], axis=1)
        s_s[pl.ds(diag + r * sub, sub), c0:] = s_top
        m_r = jnp.max(s_top, axis=0, keepdims=True)
        if r == 0:
            sm_s[...] = s[sub:]
            m_s[...] = jnp.maximum(m_r, jnp.max(s[sub:], axis=0, keepdims=True))
        else:
            m_s[:, c0:] = jnp.maximum(m_s[:, c0:], m_r)

    def score_body(t, carry):
        start = pl.multiple_of(t * chunk, chunk)
        s = _dot(with_feats(k_ref[pl.ds(start, chunk), :], 0, (start - i * tq).astype(F32) * slope),
                 qt_s[...])
        s_s[pl.ds(start, chunk), :] = s
        m_s[...] = jnp.maximum(m_s[...], jnp.max(s, axis=0, keepdims=True))
        return carry

    q_abs = jnp.abs(qt_s[:LANES, :].astype(F32))
    q_bound = jnp.sum(q_abs * jnp.tile(kabs_s[...], (1, n_cols // LANES)), axis=0, keepdims=True)
    dist = (SKIP_BELOW + jnp.max(q_bound - m_s[...])) * inv_slope
    n_chunks = i * (tq // chunk)
    n_skip = jnp.floor(i.astype(F32) * (tq / chunk) - (dist - 1.0) / chunk)
    n_skip = jnp.clip(n_skip.astype(jnp.int32), 0, n_chunks)
    lax.fori_loop(n_skip, n_chunks, score_body, 0)

    m = m_s[...]
    for r in range(n_sub):
        c0 = 2 * sub * r
        vt_r = vt_s[(diag + r * sub) // vtc][:, (r * sub) % vtc:(r * sub) % vtc + sub]
        p = jnp.exp2((s_s[pl.ds(diag + r * sub, sub), c0:] - m[:, c0:]).astype(BF16))
        if r == 0:
            p = jnp.concatenate([p, jnp.exp2((sm_s[...] - m).astype(BF16)),
                                 jnp.zeros((LANES - N_META, n_cols), BF16)], axis=0)
            acc_s[...] = _dot(jnp.concatenate([vt_r, mvt_s[...]], axis=1), p)
        else:
            acc_s[:, c0:] += _dot(vt_r, p)

    def pv_body(t, carry):
        start = pl.multiple_of(t * chunk, chunk)
        p = jnp.exp2((s_s[pl.ds(start, chunk), :] - m).astype(BF16))
        blk = start // vtc
        upd = _dot(vt_s[blk], p[:vtc])
        for c in range(1, chunk // vtc):
            upd = upd + _dot(vt_s[blk + c], p[c * vtc:(c + 1) * vtc])
        acc_s[...] += upd
        return carry

    lax.fori_loop(n_skip, n_chunks, pv_body, 0)

    lam = (jnp.exp(jnp.sum(lq1_ref[...] * lk1_ref[...], axis=1, keepdims=True))
           - jnp.exp(jnp.sum(lq2_ref[...] * lk2_ref[...], axis=1, keepdims=True)) + lam_init)
    inv_l = 1.0 / acc_s[dv:dv + 1, :]
    for c in range(n_sub):
        c0 = 2 * sub * c
        o1 = acc_s[:dv, c0:c0 + sub] * inv_l[:, c0:c0 + sub]
        o2 = acc_s[:dv, c0 + sub:c0 + 2 * sub] * inv_l[:, c0 + sub:c0 + 2 * sub]
        ot = o1 - lam * o2
        yt = ot * lax.rsqrt(jnp.mean(ot * ot, axis=0, keepdims=True) + NORM_EPS)
        o_ref[c * sub:(c + 1) * sub, :] = (yt.T * hg_ref[...] * (1.0 - lam_init)).astype(o_ref.dtype)


def _diff_attention(proj, proj_t, mk, mvt, lq1, lk1, lq2, lk2, head_g, *, batch, seq, lam_init):
    n = batch * seq
    tq = ATTN_TQ
    assert seq % tq == 0 and tq % ATTN_CHUNK == 0 and ATTN_CHUNK % ATTN_VT_COLS == 0
    nq = seq // tq
    dv = DA_VDIM
    slopes = 2.0 ** (-8.0 * (np.arange(DA_HEADS, dtype=np.float64) + 1.0) / DA_HEADS) * LOG2_E
    slopes = jnp.asarray(np.concatenate([slopes, 1.0 / slopes]), F32)
    vec = lambda: pl.BlockSpec((1, DA_HEAD_DIM), lambda b, h, i: (0, 0))
    n_cols = 2 * tq
    n_vrows = dv + BF16_SUBLANES
    return pl.pallas_call(
        functools.partial(_attn_kernel, lam_init=lam_init),
        grid=(batch, DA_HEADS, nq),
        in_specs=[
            pl.BlockSpec(memory_space=pltpu.SMEM),
            pl.BlockSpec((dv, tq), lambda b, h, i: (h, b * nq + i)),
            pl.BlockSpec((seq, dv), lambda b, h, i: (b, PROJ_COLS["k_a"] * DA_HEADS + h)),
            pl.BlockSpec((dv, seq), lambda b, h, i: (DA_HEADS + h, b)),
            pl.BlockSpec((N_META, dv), lambda b, h, i: (0, h)),
            pl.BlockSpec((dv, LANES), lambda b, h, i: (h, 0)),
            vec(), vec(), vec(), vec(),
            pl.BlockSpec((1, dv), lambda b, h, i: (0, 0)),
        ],
        out_specs=pl.BlockSpec((tq, dv), lambda b, h, i: (b * nq + i, h)),
        out_shape=jax.ShapeDtypeStruct((n, DA_HEADS * dv), BF16),
        scratch_shapes=[pltpu.VMEM((n_vrows, LANES), BF16),
                        pltpu.VMEM((seq // ATTN_VT_COLS, n_vrows, ATTN_VT_COLS), BF16),
                        pltpu.VMEM((max(tq, ATTN_CHUNK), LANES), F32),
                        pltpu.VMEM((LANES, LANES), F32),
                        pltpu.VMEM((2 * LANES, n_cols), BF16),
                        pltpu.VMEM((seq, n_cols), F32),
                        pltpu.VMEM((N_META, n_cols), F32),
                        pltpu.VMEM((1, n_cols), F32),
                        pltpu.VMEM((n_vrows, n_cols), F32)],
        compiler_params=pltpu.CompilerParams(
            dimension_semantics=("parallel", "parallel", "arbitrary"),
            vmem_limit_bytes=V7X_VMEM_LIMIT_BYTES),
    )(slopes, proj_t, proj, proj_t, mk, mvt, lq1, lk1, lq2, lk2, head_g)


def _ret_kernel(lg_ref, q_ref, k_ref, v_ref, mk_ref, mv_ref, o_ref, state_s, dec_s, *, chunk, k_scale):
    h = pl.program_id(1)
    ci = pl.program_id(2)
    lg = lg_ref[h]
    rows = lax.broadcasted_iota(jnp.int32, (chunk, 1), 0).astype(F32)
    end_dec = jnp.exp((chunk - 1.0 - rows) * lg)
    cross_dec = jnp.exp((rows + 1.0) * lg)
    chunk_dec = jnp.exp(jnp.full((1, 1), float(chunk), F32) * lg)
    ks = jnp.asarray(k_scale, BF16)

    @pl.when(ci == 0)
    def _():
        ri = lax.broadcasted_iota(jnp.int32, (chunk, chunk), 0)
        cc = lax.broadcasted_iota(jnp.int32, (chunk, chunk), 1)
        diff = (ri - cc).astype(F32)
        dec_s[...] = jnp.where(diff >= 0, jnp.exp(jnp.maximum(diff, 0.0) * lg), 0.0)
        kd = ((mk_ref[...] * ks).astype(F32) * end_dec).astype(BF16)
        state_s[...] = _dot_tn(kd, mv_ref[...])

    state = state_s[...]
    for r0 in range(0, q_ref.shape[0], chunk):
        rs = slice(r0, r0 + chunk)
        qc = q_ref[rs, :]
        kc = k_ref[rs, :] * ks
        vc = v_ref[rs, :]
        a = _dot_nt(qc, kc) * dec_s[...]
        o = _dot(a.astype(BF16), vc) + _dot(qc, state.astype(BF16)) * cross_dec
        kd = (kc.astype(F32) * end_dec).astype(BF16)
        state = state * chunk_dec + _dot_tn(kd, vc)
        o_ref[rs, :] = o.astype(o_ref.dtype)
    state_s[...] = state


def _retention(proj, mk, mv, *, batch, seq, chunk, step_rows, d_model):
    n = batch * seq
    nc = seq // step_rows
    dk = d_model // RET_HEADS
    dv = 2 * d_model // RET_HEADS
    q_blk0 = (PROJ_COLS["q_r"] * d_model) // dk
    k_blk0 = (PROJ_COLS["k_r"] * d_model) // dk
    v_blk0 = (PROJ_COLS["v_r"] * d_model) // dv
    log_gamma = jnp.asarray(np.log(1.0 - 2.0 ** (-5.0 - np.arange(RET_HEADS, dtype=np.float32))), F32)
    return pl.pallas_call(
        functools.partial(_ret_kernel, chunk=chunk, k_scale=dk ** -0.5),
        grid=(batch, RET_HEADS, nc),
        in_specs=[
            pl.BlockSpec(memory_space=pltpu.SMEM),
            pl.BlockSpec((step_rows, dk), lambda b, h, c: (b * nc + c, q_blk0 + h)),
            pl.BlockSpec((step_rows, dk), lambda b, h, c: (b * nc + c, k_blk0 + h)),
            pl.BlockSpec((step_rows, dv), lambda b, h, c: (b * nc + c, v_blk0 + h)),
            pl.BlockSpec((chunk, dk), lambda b, h, c: (0, h)),
            pl.BlockSpec((chunk, dv), lambda b, h, c: (0, h)),
        ],
        out_specs=pl.BlockSpec((step_rows, dv), lambda b, h, c: (b * nc + c, h)),
        out_shape=jax.ShapeDtypeStruct((n, RET_HEADS * dv), BF16),
        scratch_shapes=[pltpu.VMEM((dk, dv), F32), pltpu.VMEM((chunk, chunk), F32)],
        compiler_params=pltpu.CompilerParams(
            dimension_semantics=("parallel", "parallel", "arbitrary"),
            vmem_limit_bytes=V7X_VMEM_LIMIT_BYTES),
    )(log_gamma, proj, proj, proj, mk, mv)


def _merge_kernel(x_ref, yda_ref, oret_ref, gate0_ref, gate1_ref, ga_ref, gr_ref, wda_ref, wret_ref, wout_ref,
                  o_ref):
    a = _dot(yda_ref[...], wda_ref[...])
    dv = oret_ref.shape[1] // RET_HEADS
    gates = jnp.concatenate([gate0_ref[...], gate1_ref[...]], axis=1).astype(F32)
    heads = []
    for hh in range(RET_HEADS):
        g = gates[:, hh * dv:(hh + 1) * dv]
        heads.append((_rms(oret_ref[:, hh * dv:(hh + 1) * dv].astype(F32)) * (g * jax.nn.sigmoid(g))).astype(BF16))
    r = _dot(jnp.concatenate(heads, axis=1), wret_ref[...])
    merged = (jax.nn.sigmoid(ga_ref[...].astype(F32)) * a
              + jax.nn.sigmoid(gr_ref[...].astype(F32)) * r)
    o_ref[...] = x_ref[...] + _dot(merged.astype(BF16), wout_ref[...])


def _merge(x2, y_da, o_ret, proj, w_da, w_ret, w_out, *, tm):
    n, d = x2.shape
    col = lambda name, off=0: pl.BlockSpec((tm, d), lambda i: (i, PROJ_COLS[name] + off))
    const = lambda shape: pl.BlockSpec(shape, lambda i: (0, 0), pipeline_mode=pl.Buffered(1))
    return pl.pallas_call(
        _merge_kernel,
        grid=(n // tm,),
        in_specs=[
            pl.BlockSpec((tm, d), lambda i: (i, 0)),
            pl.BlockSpec((tm, y_da.shape[1]), lambda i: (i, 0)),
            pl.BlockSpec((tm, o_ret.shape[1]), lambda i: (i, 0)),
            col("gate_r"), col("gate_r", 1), col("g_a"), col("g_r"),
            const(w_da.shape), const(w_ret.shape), const(w_out.shape),
        ],
        out_specs=pl.BlockSpec((tm, d), lambda i: (i, 0)),
        out_shape=jax.ShapeDtypeStruct((n, d), F32),
        compiler_params=pltpu.CompilerParams(
            dimension_semantics=("parallel",),
            vmem_limit_bytes=V7X_VMEM_LIMIT_BYTES),
    )(x2, y_da, o_ret, proj, proj, proj, proj, w_da, w_ret, w_out)


def _mlp_kernel(h_ref, g1_ref, wup_ref, wdown_ref, g2_ref, o_ref, *, tf):
    hres = h_ref[...]
    u = (_rms(hres) * g1_ref[...]).astype(BF16)
    acc = hres
    for f in range(0, wup_ref.shape[1], tf):
        a = jnp.maximum(_dot(u, wup_ref[:, f:f + tf]), 0.0)
        acc = acc + _dot((a * a).astype(BF16), wdown_ref[f:f + tf, :])
    o_ref[...] = _rms(acc) * g2_ref[...]


def _mlp(h1, g1, w_up, w_down, g2, *, tm, tf):
    n, d = h1.shape
    const = lambda shape: pl.BlockSpec(shape, lambda i: (0, 0), pipeline_mode=pl.Buffered(1))
    return pl.pallas_call(
        functools.partial(_mlp_kernel, tf=tf),
        grid=(n // tm,),
        in_specs=[
            pl.BlockSpec((tm, d), lambda i: (i, 0)),
            pl.BlockSpec((1, d), lambda i: (0, 0)),
            const(w_up.shape), const(w_down.shape),
            pl.BlockSpec((1, d), lambda i: (0, 0)),
        ],
        out_specs=pl.BlockSpec((tm, d), lambda i: (i, 0)),
        out_shape=jax.ShapeDtypeStruct((n, d), F32),
        compiler_params=pltpu.CompilerParams(
            dimension_semantics=("parallel",),
            vmem_limit_bytes=V7X_VMEM_LIMIT_BYTES),
    )(h1, g1, w_up, w_down, g2)


def _tile(n, target):
    t = min(n, target)
    assert n % t == 0, (n, t)
    return t


def kernel(x, meta_tokens, norm_mix_g, w_in, da_lambda_q1, da_lambda_k1, da_lambda_q2, da_lambda_k2,
           da_head_g, w_proj_da, w_proj_ret, w_out, norm_mlp_g, w_up, w_down, final_norm_g):
    batch, seq, d = x.shape
    assert w_in.shape[0] == 1, "single-layer block"
    assert d // RET_HEADS == 2 * LANES and DA_HEADS * DA_VDIM == d
    n = batch * seq
    layer = 0
    lam_init = 0.8 - 0.6 * math.exp(-0.3 * layer)

    x2 = x.reshape(n, d)
    w_in_b = w_in[layer].astype(BF16)
    g_mix = norm_mix_g[layer].reshape(1, d)

    w_rows = jnp.concatenate([w_in_b[:, d:2 * d], w_in_b[:, 3 * d:]], axis=1)
    w_feat = jnp.concatenate([w_in_b[:, :d], w_in_b[:, 2 * d:3 * d]], axis=1).T
    proj, proj_t = _inproj_t(x2, g_mix, w_rows, w_feat, tm=_tile(n, 512), col_group=_tile(w_rows.shape[1], 2304),
                             n_scaled=d, scale=DA_HEAD_DIM ** -0.5 * LOG2_E)
    proj_m = _inproj(meta_tokens.astype(x.dtype), g_mix, w_in_b, tm=N_META, tn=_tile(w_in_b.shape[1], 1024))

    mvt_a = jnp.pad(proj_m[:, 2 * d:3 * d].T, ((0, 0), (0, LANES - N_META)))
    row = lambda v: v[layer].reshape(1, -1).astype(F32)
    y_da = _diff_attention(proj, proj_t, proj_m[:, d:2 * d], mvt_a, row(da_lambda_q1), row(da_lambda_k1),
                           row(da_lambda_q2), row(da_lambda_k2), row(da_head_g), batch=batch, seq=seq,
                           lam_init=lam_init)

    chunk = _tile(seq, 256)
    pad_r = ((chunk - N_META, 0), (0, 0))
    mk_r = jnp.pad(proj_m[:, 4 * d:5 * d], pad_r)
    mv_r = jnp.pad(proj_m[:, 5 * d:7 * d], pad_r)
    o_ret = _retention(proj, mk_r, mv_r, batch=batch, seq=seq, chunk=chunk, step_rows=_tile(seq, 4096),
                       d_model=d)

    h1 = _merge(x2, y_da, o_ret, proj, w_proj_da[layer].astype(BF16), w_proj_ret[layer].astype(BF16),
                w_out[layer].astype(BF16), tm=_tile(n, 512))
    out = _mlp(h1, norm_mlp_g[layer].reshape(1, d), w_up[layer].astype(BF16), w_down[layer].astype(BF16),
               final_norm_g.reshape(1, d), tm=_tile(n, 1024), tf=1024)
    return out.reshape(batch, seq, d)
```

```python
import functools
import math

import numpy as np
import jax
import jax.numpy as jnp
from jax import lax
from jax.experimental import pallas as pl
from jax.experimental.pallas import tpu as pltpu

N_META = 16
DA_HEADS = 8
DA_HEAD_DIM = 64
DA_VDIM = 2 * DA_HEAD_DIM
RET_HEADS = 4
NORM_EPS = 1e-6

F32 = jnp.float32
BF16 = jnp.bfloat16

V7X_VMEM_LIMIT_BYTES = 56 * 1024 * 1024
LANES = 128
BF16_SUBLANES = 16


def _dot(a, b):
    return jnp.dot(a, b, preferred_element_type=F32)


def _dot_nt(a, b):
    return lax.dot_general(a, b, (((1,), (1,)), ((), ())), preferred_element_type=F32)


def _dot_tn(a, b):
    return lax.dot_general(a, b, (((0,), (0,)), ((), ())), preferred_element_type=F32)


def _rms(xf):
    return xf * lax.rsqrt(jnp.mean(xf * xf, axis=-1, keepdims=True) + NORM_EPS)


PROJ_COLS = {"k_a": 0, "q_r": 1, "k_r": 2, "v_r": 3, "gate_r": 5, "g_a": 7, "g_r": 8}


def _inproj_kernel(x_ref, g_ref, w_ref, o_ref, u_ref):
    @pl.when(pl.program_id(1) == 0)
    def _():
        u_ref[...] = (_rms(x_ref[...]) * g_ref[...]).astype(BF16)

    o_ref[...] = _dot(u_ref[...], w_ref[...]).astype(o_ref.dtype)


def _inproj_t_kernel(x_ref, g_ref, w_ref, wt_ref, o_ref, ot_ref, *, n_scaled, scale, col_group):
    u = (_rms(x_ref[...]) * g_ref[...]).astype(BF16)
    ot = _dot_nt(wt_ref[...], u)
    ot_ref[:n_scaled, :] = (ot[:n_scaled] * scale).astype(ot_ref.dtype)
    ot_ref[n_scaled:, :] = ot[n_scaled:].astype(ot_ref.dtype)
    for c0 in range(0, w_ref.shape[1], col_group):
        o_ref[:, c0:c0 + col_group] = _dot(u, w_ref[:, c0:c0 + col_group]).astype(o_ref.dtype)


def _inproj(x2, g, w, *, tm, tn):
    n, d = x2.shape
    d_in = w.shape[1]
    return pl.pallas_call(
        _inproj_kernel,
        grid=(n // tm, d_in // tn),
        in_specs=[
            pl.BlockSpec((tm, d), lambda i, j: (i, 0)),
            pl.BlockSpec((1, d), lambda i, j: (0, 0)),
            pl.BlockSpec((d, tn), lambda i, j: (0, j)),
        ],
        out_specs=pl.BlockSpec((tm, tn), lambda i, j: (i, j)),
        out_shape=jax.ShapeDtypeStruct((n, d_in), BF16),
        scratch_shapes=[pltpu.VMEM((tm, d), BF16)],
        compiler_params=pltpu.CompilerParams(
            dimension_semantics=("parallel", "arbitrary"),
            vmem_limit_bytes=V7X_VMEM_LIMIT_BYTES),
    )(x2, g, w)


def _inproj_t(x2, g, w, wt, *, tm, col_group, n_scaled, scale):
    n, d = x2.shape
    d_in = w.shape[1]
    d_t = wt.shape[0]
    const = lambda shape: pl.BlockSpec(shape, lambda i: (0, 0), pipeline_mode=pl.Buffered(1))
    return pl.pallas_call(
        functools.partial(_inproj_t_kernel, n_scaled=n_scaled, scale=scale, col_group=col_group),
        grid=(n // tm,),
        in_specs=[
            pl.BlockSpec((tm, d), lambda i: (i, 0)),
            pl.BlockSpec((1, d), lambda i: (0, 0)),
            const(w.shape), const(wt.shape),
        ],
        out_specs=[pl.BlockSpec((tm, d_in), lambda i: (i, 0)),
                   pl.BlockSpec((d_t, tm), lambda i: (0, i))],
        out_shape=[jax.ShapeDtypeStruct((n, d_in), BF16), jax.ShapeDtypeStruct((d_t, n), BF16)],
        compiler_params=pltpu.CompilerParams(
            dimension_semantics=("parallel",),
            vmem_limit_bytes=V7X_VMEM_LIMIT_BYTES),
    )(x2, g, w, wt)


ATTN_TQ = 1024
ATTN_SUB = 256
ATTN_CHUNK = 1024
ATTN_VT_COLS = 512
N_FEAT = 4
LOG2_E = math.log2(math.e)
SKIP_BELOW = 140.0


def _attn_kernel(slopes_ref, qt_ref, k_ref, vt_ref, mk_ref, mvt_ref, lq1_ref, lk1_ref, lq2_ref, lk2_ref,
                 hg_ref, o_ref, mvt_s, vt_s, kpos_s, kabs_s, qt_s, s_s, sm_s, m_s, acc_s, *, lam_init):
    h = pl.program_id(1)
    i = pl.program_id(2)
    slope = slopes_ref[h]
    inv_slope = slopes_ref[DA_HEADS + h]
    tq, sub, chunk, vtc = ATTN_TQ, ATTN_SUB, ATTN_CHUNK, ATTN_VT_COLS
    n_sub = tq // sub
    dv = vt_ref.shape[0]
    n_cols = 2 * tq
    neg_inf = jnp.float32(-jnp.inf)
    lane_row = lax.broadcasted_iota(jnp.int32, (1, LANES), 1)

    @pl.when(i == 0)
    def _():
        n_ones = vt_s.shape[1] - dv
        for c in range(vt_s.shape[0]):
            vt_s[c, :dv, :] = vt_ref[:, c * vtc:(c + 1) * vtc]
            vt_s[c, dv:, :] = jnp.ones((n_ones, vtc), vt_s.dtype)
        mvt_s[:dv, :] = mvt_ref[...]
        mvt_s[dv:, :] = jnp.ones((n_ones, LANES), mvt_s.dtype)
        pos = lax.broadcasted_iota(jnp.int32, kpos_s.shape, 0).astype(F32) * slope
        plane = lax.broadcasted_iota(jnp.int32, kpos_s.shape, 1)
        pos_hi = pos.astype(BF16).astype(F32)
        kpos_s[...] = jnp.where(plane == 0, pos_hi, jnp.where(plane == 1, pos - pos_hi, 0.0))
        frow = lax.broadcasted_iota(jnp.int32, (LANES, n_cols), 0)
        qt_s[:LANES, :] = jnp.zeros((LANES, n_cols), qt_s.dtype)
        qt_s[LANES:, :] = jnp.where(frow < N_FEAT, 1.0, 0.0).astype(qt_s.dtype)
        kabs = jnp.max(jnp.abs(k_ref[...].astype(F32)), axis=0, keepdims=True)
        kabs_s[...] = jnp.broadcast_to(kabs, kabs_s.shape).T

    for c in range(n_sub):
        qt_s[:DA_HEAD_DIM, 2 * c * sub:(2 * c + 1) * sub] = qt_ref[:DA_HEAD_DIM, c * sub:(c + 1) * sub]
        qt_s[DA_HEAD_DIM:LANES, (2 * c + 1) * sub:(2 * c + 2) * sub] = qt_ref[DA_HEAD_DIM:, c * sub:(c + 1) * sub]

    q_prod = jnp.abs(qt_ref[...].astype(F32)) * jnp.tile(kabs_s[...], (1, tq // LANES))
    b1 = jnp.sum(q_prod[:DA_HEAD_DIM], axis=0, keepdims=True)
    b2 = jnp.sum(q_prod[DA_HEAD_DIM:], axis=0, keepdims=True)
    q_bound = jnp.concatenate(
        [b[:, c * sub:(c + 1) * sub] for c in range(n_sub) for b in (b1, b2)], axis=1)

    def key_feats(row0, n_rows, offset):
        off = jnp.full((1, LANES), offset, F32)
        off_hi = off.astype(BF16).astype(F32)
        off_row = jnp.where(lane_row == 2, off_hi, jnp.where(lane_row == 3, off - off_hi, 0.0))
        return (kpos_s[row0:row0 + n_rows, :] + off_row).astype(BF16)

    def with_feats(k_blk, row0, offset):
        return jnp.concatenate([k_blk, key_feats(row0, k_blk.shape[0], offset)], axis=1)

    diag = pl.multiple_of(i * tq, tq)
    kr = lax.broadcasted_iota(jnp.int32, (sub, 2 * sub), 0)
    qc = lax.broadcasted_iota(jnp.int32, (sub, 2 * sub), 1)
    causal = kr <= (qc & (sub - 1))
    for r in range(n_sub):
        c0 = 2 * sub * r
        lhs = with_feats(k_ref[pl.ds(diag + r * sub, sub), :], r * sub, 0.0)
        if r == 0:
            lhs = jnp.concatenate(
                [lhs, with_feats(mk_ref[...], 0, -(i * tq + N_META).astype(F32) * slope)], axis=0)
        s = _dot(lhs, qt_s[:, c0:])
        s_first = jnp.where(causal, s[:sub, :2 * sub], neg_inf)
        s_top = s_first if r == n_sub - 1 else jnp.concatenate([s_first, s[:sub, 2 * sub:]], axis=1)
        s_s[pl.ds(diag + r * sub, sub), c0:] = s_top
        m_r = jnp.max(s_top, axis=0, keepdims=True)
        if r == 0:
            sm_s[...] = s[sub:]
            m_s[...] = jnp.maximum(m_r, jnp.max(s[sub:], axis=0, keepdims=True))
        else:
            m_s[:, c0:] = jnp.maximum(m_s[:, c0:], m_r)

    def score_body(t, carry):
        start = pl.multiple_of(t * chunk, chunk)
        s = _dot(with_feats(k_ref[pl.ds(start, chunk), :], 0, (start - i * tq).astype(F32) * slope),
                 qt_s[...])
        s_s[pl.ds(start, chunk), :] = s
        m_s[...] = jnp.maximum(m_s[...], jnp.max(s, axis=0, keepdims=True))
        return carry

    dist = (SKIP_BELOW + jnp.max(q_bound - m_s[...])) * inv_slope
    n_chunks = i * (tq // chunk)
    n_skip = jnp.floor(i.astype(F32) * (tq / chunk) - (dist - 1.0) / chunk)
    n_skip = jnp.clip(n_skip.astype(jnp.int32), 0, n_chunks)
    lax.fori_loop(n_skip, n_chunks, score_body, 0)

    m = m_s[...]
    for r in range(n_sub):
        c0 = 2 * sub * r
        vt_r = vt_s[(diag + r * sub) // vtc][:, (r * sub) % vtc:(r * sub) % vtc + sub]
        p = jnp.exp2((s_s[pl.ds(diag + r * sub, sub), c0:] - m[:, c0:]).astype(BF16))
        if r == 0:
            p = jnp.concatenate([p, jnp.exp2((sm_s[...] - m).astype(BF16)),
                                 jnp.zeros((LANES - N_META, n_cols), BF16)], axis=0)
            acc_s[...] = _dot(jnp.concatenate([vt_r, mvt_s[...]], axis=1), p)
        else:
            acc_s[:, c0:] += _dot(vt_r, p)

    def pv_body(t, carry):
        start = pl.multiple_of(t * chunk, chunk)
        p = jnp.exp2((s_s[pl.ds(start, chunk), :] - m).astype(BF16))
        blk = start // vtc
        upd = _dot(vt_s[blk], p[:vtc])
        for c in range(1, chunk // vtc):
            upd = upd + _dot(vt_s[blk + c], p[c * vtc:(c + 1) * vtc])
        acc_s[...] += upd
        return carry

    lax.fori_loop(n_skip, n_chunks, pv_body, 0)

    lam = (jnp.exp(jnp.sum(lq1_ref[...] * lk1_ref[...], axis=1, keepdims=True))
           - jnp.exp(jnp.sum(lq2_ref[...] * lk2_ref[...], axis=1, keepdims=True)) + lam_init)
    inv_l = 1.0 / acc_s[dv:dv + 1, :]
    for c in range(n_sub):
        c0 = 2 * sub * c
        o1 = acc_s[:dv, c0:c0 + sub] * inv_l[:, c0:c0 + sub]
        o2 = acc_s[:dv, c0 + sub:c0 + 2 * sub] * inv_l[:, c0 + sub:c0 + 2 * sub]
        ot = o1 - lam * o2
        yt = ot * lax.rsqrt(jnp.mean(ot * ot, axis=0, keepdims=True) + NORM_EPS)
        o_ref[c * sub:(c + 1) * sub, :] = (yt.T * hg_ref[...] * (1.0 - lam_init)).astype(o_ref.dtype)


def _diff_attention(proj, proj_t, mk, mvt, lq1, lk1, lq2, lk2, head_g, *, batch, seq, lam_init):
    n = batch * seq
    tq = ATTN_TQ
    assert seq % tq == 0 and tq % ATTN_CHUNK == 0 and ATTN_CHUNK % ATTN_VT_COLS == 0
    nq = seq // tq
    dv = DA_VDIM
    slopes = 2.0 ** (-8.0 * (np.arange(DA_HEADS, dtype=np.float64) + 1.0) / DA_HEADS) * LOG2_E
    slopes = jnp.asarray(np.concatenate([slopes, 1.0 / slopes]), F32)
    vec = lambda: pl.BlockSpec((1, DA_HEAD_DIM), lambda b, h, i: (0, 0))
    n_cols = 2 * tq
    n_vrows = dv + BF16_SUBLANES
    return pl.pallas_call(
        functools.partial(_attn_kernel, lam_init=lam_init),
        grid=(batch, DA_HEADS, nq),
        in_specs=[
            pl.BlockSpec(memory_space=pltpu.SMEM),
            pl.BlockSpec((dv, tq), lambda b, h, i: (h, b * nq + i)),
            pl.BlockSpec((seq, dv), lambda b, h, i: (b, PROJ_COLS["k_a"] * DA_HEADS + h)),
            pl.BlockSpec((dv, seq), lambda b, h, i: (DA_HEADS + h, b)),
            pl.BlockSpec((N_META, dv), lambda b, h, i: (0, h)),
            pl.BlockSpec((dv, LANES), lambda b, h, i: (h, 0)),
            vec(), vec(), vec(), vec(),
            pl.BlockSpec((1, dv), lambda b, h, i: (0, 0)),
        ],
        out_specs=pl.BlockSpec((tq, dv), lambda b, h, i: (b * nq + i, h)),
        out_shape=jax.ShapeDtypeStruct((n, DA_HEADS * dv), BF16),
        scratch_shapes=[pltpu.VMEM((n_vrows, LANES), BF16),
                        pltpu.VMEM((seq // ATTN_VT_COLS, n_vrows, ATTN_VT_COLS), BF16),
                        pltpu.VMEM((max(tq, ATTN_CHUNK), LANES), F32),
                        pltpu.VMEM((LANES, LANES), F32),
                        pltpu.VMEM((2 * LANES, n_cols), BF16),
                        pltpu.VMEM((seq, n_cols), F32),
                        pltpu.VMEM((N_META, n_cols), F32),
                        pltpu.VMEM((1, n_cols), F32),
                        pltpu.VMEM((n_vrows, n_cols), F32)],
        compiler_params=pltpu.CompilerParams(
            dimension_semantics=("parallel", "parallel", "arbitrary"),
            vmem_limit_bytes=V7X_VMEM_LIMIT_BYTES),
    )(slopes, proj_t, proj, proj_t, mk, mvt, lq1, lk1, lq2, lk2, head_g)


def _ret_merge_kernel(lg_ref, x_ref, yda_ref, q_ref, k_ref, v0_ref, v1_ref, gate0_ref, gate1_ref, ga_ref, gr_ref,
                      mk_ref, mv0_ref, mv1_ref, wda_ref, wret_ref, wout_ref, o_ref, state_s, dec_s,
                      *, chunk, k_scale):
    ci = pl.program_id(1)
    n_heads = state_s.shape[0]
    dk, dv = state_s.shape[1], state_s.shape[2]
    rows = lax.broadcasted_iota(jnp.int32, (chunk, 1), 0).astype(F32)
    ks = jnp.asarray(k_scale, BF16)
    v_refs, mv_refs, gate_refs = (v0_ref, v1_ref), (mv0_ref, mv1_ref), (gate0_ref, gate1_ref)
    heads_per_ref = v0_ref.shape[1] // dv

    def head_cols(refs, hh):
        return refs[hh // heads_per_ref], slice((hh % heads_per_ref) * dv, (hh % heads_per_ref + 1) * dv)

    @pl.when(ci == 0)
    def _():
        ri = lax.broadcasted_iota(jnp.int32, (chunk, chunk), 0)
        cc = lax.broadcasted_iota(jnp.int32, (chunk, chunk), 1)
        diff = (ri - cc).astype(F32)
        for hh in range(n_heads):
            lg = lg_ref[hh]
            dec_s[hh] = jnp.where(diff >= 0, jnp.exp(jnp.maximum(diff, 0.0) * lg), 0.0)
            kd = ((mk_ref[:, hh * dk:(hh + 1) * dk] * ks).astype(F32) * jnp.exp((chunk - 1.0 - rows) * lg))
            mv_ref, cols = head_cols(mv_refs, hh)
            state_s[hh] = _dot_tn(kd.astype(BF16), mv_ref[:, cols])

    y_heads = []
    for hh in range(n_heads):
        lg = lg_ref[hh]
        end_dec = jnp.exp((chunk - 1.0 - rows) * lg)
        cross_dec = jnp.exp((rows + 1.0) * lg)
        chunk_dec = jnp.exp(jnp.full((1, 1), float(chunk), F32) * lg)
        v_ref, vcols = head_cols(v_refs, hh)
        gate_ref, gcols = head_cols(gate_refs, hh)
        state = state_s[hh]
        y_chunks = []
        for r0 in range(0, x_ref.shape[0], chunk):
            rs = slice(r0, r0 + chunk)
            qc = q_ref[rs, hh * dk:(hh + 1) * dk]
            kc = k_ref[rs, hh * dk:(hh + 1) * dk] * ks
            vc = v_ref[rs, vcols]
            a = _dot_nt(qc, kc) * dec_s[hh]
            o = _dot(a.astype(BF16), vc) + _dot(qc, state.astype(BF16)) * cross_dec
            kd = (kc.astype(F32) * end_dec).astype(BF16)
            state = state * chunk_dec + _dot_tn(kd, vc)
            g = gate_ref[rs, gcols].astype(F32)
            y_chunks.append((_rms(o) * (g * jax.nn.sigmoid(g))).astype(BF16))
        state_s[hh] = state
        y_heads.append(jnp.concatenate(y_chunks, axis=0))
    y_ret = jnp.concatenate(y_heads, axis=1)

    a = _dot(yda_ref[...], wda_ref[...])
    r = _dot(y_ret, wret_ref[...])
    merged = (jax.nn.sigmoid(ga_ref[...].astype(F32)) * a
              + jax.nn.sigmoid(gr_ref[...].astype(F32)) * r)
    o_ref[...] = x_ref[...] + _dot(merged.astype(BF16), wout_ref[...])


def _ret_merge(x2, y_da, proj, mk, mv, w_da, w_ret, w_out, *, batch, seq, chunk, step_rows):
    n, d = x2.shape
    nc = seq // step_rows
    dk = d // RET_HEADS
    dv = 2 * d // RET_HEADS
    log_gamma = jnp.asarray(np.log(1.0 - 2.0 ** (-5.0 - np.arange(RET_HEADS, dtype=np.float32))), F32)
    row_tile = lambda width, col: pl.BlockSpec((step_rows, width), lambda b, c: (b * nc + c, col))
    col = lambda name, off=0: row_tile(d, PROJ_COLS[name] + off)
    meta = lambda off: pl.BlockSpec((chunk, d), lambda b, c: (0, off))
    const = lambda shape: pl.BlockSpec(shape, lambda b, c: (0, 0), pipeline_mode=pl.Buffered(1))
    return pl.pallas_call(
        functools.partial(_ret_merge_kernel, chunk=chunk, k_scale=dk ** -0.5),
        grid=(batch, nc),
        in_specs=[
            pl.BlockSpec(memory_space=pltpu.SMEM),
            row_tile(d, 0),
            row_tile(d, 0),
            col("q_r"), col("k_r"), col("v_r"), col("v_r", 1), col("gate_r"), col("gate_r", 1),
            col("g_a"), col("g_r"),
            meta(0), meta(0), meta(1),
            const(w_da.shape), const(w_ret.shape), const(w_out.shape),
        ],
        out_specs=row_tile(d, 0),
        out_shape=jax.ShapeDtypeStruct((n, d), F32),
        scratch_shapes=[pltpu.VMEM((RET_HEADS, dk, dv), F32), pltpu.VMEM((RET_HEADS, chunk, chunk), F32)],
        compiler_params=pltpu.CompilerParams(
            dimension_semantics=("parallel", "arbitrary"),
            vmem_limit_bytes=V7X_VMEM_LIMIT_BYTES),
    )(log_gamma, x2, y_da, proj, proj, proj, proj, proj, proj, proj, proj, mk, mv, mv, w_da, w_ret, w_out)


def _mlp_kernel(h_ref, g1_ref, wup_ref, wdown_ref, g2_ref, o_ref, *, tf):
    hres = h_ref[...]
    u = (_rms(hres) * g1_ref[...]).astype(BF16)
    acc = hres
    for f in range(0, wup_ref.shape[1], tf):
        a = jnp.maximum(_dot(u, wup_ref[:, f:f + tf]), 0.0)
        acc = acc + _dot((a * a).astype(BF16), wdown_ref[f:f + tf, :])
    o_ref[...] = _rms(acc) * g2_ref[...]


def _mlp(h1, g1, w_up, w_down, g2, *, tm, tf):
    n, d = h1.shape
    const = lambda shape: pl.BlockSpec(shape, lambda i: (0, 0), pipeline_mode=pl.Buffered(1))
    return pl.pallas_call(
        functools.partial(_mlp_kernel, tf=tf),
        grid=(n // tm,),
        in_specs=[
            pl.BlockSpec((tm, d), lambda i: (i, 0)),
            pl.BlockSpec((1, d), lambda i: (0, 0)),
            const(w_up.shape), const(w_down.shape),
            pl.BlockSpec((1, d), lambda i: (0, 0)),
        ],
        out_specs=pl.BlockSpec((tm, d), lambda i: (i, 0)),
        out_shape=jax.ShapeDtypeStruct((n, d), F32),
        compiler_params=pltpu.CompilerParams(
            dimension_semantics=("parallel",),
            vmem_limit_bytes=V7X_VMEM_LIMIT_BYTES),
    )(h1, g1, w_up, w_down, g2)


def _tile(n, target):
    t = min(n, target)
    assert n % t == 0, (n, t)
    return t


def kernel(x, meta_tokens, norm_mix_g, w_in, da_lambda_q1, da_lambda_k1, da_lambda_q2, da_lambda_k2,
           da_head_g, w_proj_da, w_proj_ret, w_out, norm_mlp_g, w_up, w_down, final_norm_g):
    batch, seq, d = x.shape
    assert w_in.shape[0] == 1, "single-layer block"
    assert d // RET_HEADS == 2 * LANES and DA_HEADS * DA_VDIM == d
    n = batch * seq
    layer = 0
    lam_init = 0.8 - 0.6 * math.exp(-0.3 * layer)

    x2 = x.reshape(n, d)
    w_in_b = w_in[layer].astype(BF16)
    g_mix = norm_mix_g[layer].reshape(1, d)

    w_rows = jnp.concatenate([w_in_b[:, d:2 * d], w_in_b[:, 3 * d:]], axis=1)
    w_feat = jnp.concatenate([w_in_b[:, :d], w_in_b[:, 2 * d:3 * d]], axis=1).T
    proj, proj_t = _inproj_t(x2, g_mix, w_rows, w_feat, tm=_tile(n, 512), col_group=_tile(w_rows.shape[1], 2304),
                             n_scaled=d, scale=DA_HEAD_DIM ** -0.5 * LOG2_E)
    proj_m = _inproj(meta_tokens.astype(x.dtype), g_mix, w_in_b, tm=N_META, tn=_tile(w_in_b.shape[1], 1024))

    mvt_a = jnp.pad(proj_m[:, 2 * d:3 * d].T, ((0, 0), (0, LANES - N_META)))
    row = lambda v: v[layer].reshape(1, -1).astype(F32)
    y_da = _diff_attention(proj, proj_t, proj_m[:, d:2 * d], mvt_a, row(da_lambda_q1), row(da_lambda_k1),
                           row(da_lambda_q2), row(da_lambda_k2), row(da_head_g), batch=batch, seq=seq,
                           lam_init=lam_init)

    chunk = _tile(seq, 256)
    pad_r = ((chunk - N_META, 0), (0, 0))
    mk_r = jnp.pad(proj_m[:, 4 * d:5 * d], pad_r)
    mv_r = jnp.pad(proj_m[:, 5 * d:7 * d], pad_r)
    h1 = _ret_merge(x2, y_da, proj, mk_r, mv_r, w_proj_da[layer].astype(BF16), w_proj_ret[layer].astype(BF16),
                    w_out[layer].astype(BF16), batch=batch, seq=seq, chunk=chunk, step_rows=_tile(seq, 512))
    out = _mlp(h1, norm_mlp_g[layer].reshape(1, d), w_up[layer].astype(BF16), w_down[layer].astype(BF16),
               final_norm_g.reshape(1, d), tm=_tile(n, 1024), tf=1024)
    return out.reshape(batch, seq, d)
```

```python
import functools
import math

import numpy as np
import jax
import jax.numpy as jnp
from jax import lax
from jax.experimental import pallas as pl
from jax.experimental.pallas import tpu as pltpu

N_META = 16
DA_HEADS = 8
DA_HEAD_DIM = 64
DA_VDIM = 2 * DA_HEAD_DIM
RET_HEADS = 4
NORM_EPS = 1e-6

F32 = jnp.float32
BF16 = jnp.bfloat16

V7X_VMEM_LIMIT_BYTES = 56 * 1024 * 1024
LANES = 128
BF16_SUBLANES = 16


def _dot(a, b):
    return jnp.dot(a, b, preferred_element_type=F32)


def _dot_nt(a, b):
    return lax.dot_general(a, b, (((1,), (1,)), ((), ())), preferred_element_type=F32)


def _dot_tn(a, b):
    return lax.dot_general(a, b, (((0,), (0,)), ((), ())), preferred_element_type=F32)


def _rms(xf):
    return xf * lax.rsqrt(jnp.mean(xf * xf, axis=-1, keepdims=True) + NORM_EPS)


PROJ_COLS = {"k_a": 0, "q_r": 1, "k_r": 2, "v_r": 3, "gate_r": 5, "g_a": 7, "g_r": 8}


def _inproj_kernel(x_ref, g_ref, w_ref, o_ref, u_ref):
    @pl.when(pl.program_id(1) == 0)
    def _():
        u_ref[...] = (_rms(x_ref[...]) * g_ref[...]).astype(BF16)

    o_ref[...] = _dot(u_ref[...], w_ref[...]).astype(o_ref.dtype)


def _inproj_t_kernel(x_ref, g_ref, w_ref, wt_ref, o_ref, ot_ref, *, n_scaled, scale, col_group):
    u = (_rms(x_ref[...]) * g_ref[...]).astype(BF16)
    ot = _dot_nt(wt_ref[...], u)
    ot_ref[:n_scaled, :] = (ot[:n_scaled] * scale).astype(ot_ref.dtype)
    ot_ref[n_scaled:, :] = ot[n_scaled:].astype(ot_ref.dtype)
    for c0 in range(0, w_ref.shape[1], col_group):
        o_ref[:, c0:c0 + col_group] = _dot(u, w_ref[:, c0:c0 + col_group]).astype(o_ref.dtype)


def _inproj(x2, g, w, *, tm, tn):
    n, d = x2.shape
    d_in = w.shape[1]
    return pl.pallas_call(
        _inproj_kernel,
        grid=(n // tm, d_in // tn),
        in_specs=[
            pl.BlockSpec((tm, d), lambda i, j: (i, 0)),
            pl.BlockSpec((1, d), lambda i, j: (0, 0)),
            pl.BlockSpec((d, tn), lambda i, j: (0, j)),
        ],
        out_specs=pl.BlockSpec((tm, tn), lambda i, j: (i, j)),
        out_shape=jax.ShapeDtypeStruct((n, d_in), BF16),
        scratch_shapes=[pltpu.VMEM((tm, d), BF16)],
        compiler_params=pltpu.CompilerParams(
            dimension_semantics=("parallel", "arbitrary"),
            vmem_limit_bytes=V7X_VMEM_LIMIT_BYTES),
    )(x2, g, w)


def _inproj_t(x2, g, w, wt, *, tm, col_group, n_scaled, scale):
    n, d = x2.shape
    d_in = w.shape[1]
    d_t = wt.shape[0]
    const = lambda shape: pl.BlockSpec(shape, lambda i: (0, 0), pipeline_mode=pl.Buffered(1))
    return pl.pallas_call(
        functools.partial(_inproj_t_kernel, n_scaled=n_scaled, scale=scale, col_group=col_group),
        grid=(n // tm,),
        in_specs=[
            pl.BlockSpec((tm, d), lambda i: (i, 0)),
            pl.BlockSpec((1, d), lambda i: (0, 0)),
            const(w.shape), const(wt.shape),
        ],
        out_specs=[pl.BlockSpec((tm, d_in), lambda i: (i, 0)),
                   pl.BlockSpec((d_t, tm), lambda i: (0, i))],
        out_shape=[jax.ShapeDtypeStruct((n, d_in), BF16), jax.ShapeDtypeStruct((d_t, n), BF16)],
        compiler_params=pltpu.CompilerParams(
            dimension_semantics=("parallel",),
            vmem_limit_bytes=V7X_VMEM_LIMIT_BYTES),
    )(x2, g, w, wt)


ATTN_TQ = 1024
ATTN_SUB = 256
ATTN_CHUNK = 1024
ATTN_VT_COLS = 512
ATTN_PIECE = 256
N_FEAT = 4
LOG2_E = math.log2(math.e)
SKIP_BELOW = 140.0


def _attn_kernel(slopes_ref, qt_ref, k_ref, vt_ref, mk_ref, mvt_ref, lq1_ref, lk1_ref, lq2_ref, lk2_ref,
                 hg_ref, o_ref, mvt_s, vt_s, vtp_s, kpos_s, kabs_s, qt_s, s_s, sm_s, m_s, acc_s, *, lam_init):
    h = pl.program_id(1)
    i = pl.program_id(2)
    slope = slopes_ref[h]
    inv_slope = slopes_ref[DA_HEADS + h]
    tq, sub, chunk, vtc, piece = ATTN_TQ, ATTN_SUB, ATTN_CHUNK, ATTN_VT_COLS, ATTN_PIECE
    n_sub = tq // sub
    dv = vt_ref.shape[0]
    n_cols = 2 * tq
    neg_inf = jnp.float32(-jnp.inf)
    lane_row = lax.broadcasted_iota(jnp.int32, (1, LANES), 1)

    @pl.when(i == 0)
    def _():
        n_ones = vt_s.shape[1] - dv
        for c in range(vt_s.shape[0]):
            vt_s[c, :dv, :] = vt_ref[:, c * vtc:(c + 1) * vtc]
            vt_s[c, dv:, :] = jnp.ones((n_ones, vtc), vt_s.dtype)
        for c in range(vtp_s.shape[0]):
            vtp_s[c, :dv, :] = vt_ref[:, c * piece:(c + 1) * piece]
            vtp_s[c, dv:, :] = jnp.ones((n_ones, piece), vtp_s.dtype)
        mvt_s[:dv, :] = mvt_ref[...]
        mvt_s[dv:, :] = jnp.ones((n_ones, LANES), mvt_s.dtype)
        pos = lax.broadcasted_iota(jnp.int32, kpos_s.shape, 0).astype(F32) * slope
        plane = lax.broadcasted_iota(jnp.int32, kpos_s.shape, 1)
        pos_hi = pos.astype(BF16).astype(F32)
        kpos_s[...] = jnp.where(plane == 0, pos_hi, jnp.where(plane == 1, pos - pos_hi, 0.0))
        frow = lax.broadcasted_iota(jnp.int32, (LANES, n_cols), 0)
        qt_s[:LANES, :] = jnp.zeros((LANES, n_cols), qt_s.dtype)
        qt_s[LANES:, :] = jnp.where(frow < N_FEAT, 1.0, 0.0).astype(qt_s.dtype)
        kabs = jnp.max(jnp.abs(k_ref[...].astype(F32)), axis=0, keepdims=True)
        kabs_s[...] = jnp.broadcast_to(kabs, kabs_s.shape).T

    for c in range(n_sub):
        qt_s[:DA_HEAD_DIM, 2 * c * sub:(2 * c + 1) * sub] = qt_ref[:DA_HEAD_DIM, c * sub:(c + 1) * sub]
        qt_s[DA_HEAD_DIM:LANES, (2 * c + 1) * sub:(2 * c + 2) * sub] = qt_ref[DA_HEAD_DIM:, c * sub:(c + 1) * sub]

    q_prod = jnp.abs(qt_ref[...].astype(F32)) * jnp.tile(kabs_s[...], (1, tq // LANES))
    b1 = jnp.sum(q_prod[:DA_HEAD_DIM], axis=0, keepdims=True)
    b2 = jnp.sum(q_prod[DA_HEAD_DIM:], axis=0, keepdims=True)
    q_bound = jnp.concatenate(
        [b[:, c * sub:(c + 1) * sub] for c in range(n_sub) for b in (b1, b2)], axis=1)

    def key_feats(row0, n_rows, offset):
        off = jnp.full((1, LANES), offset, F32)
        off_hi = off.astype(BF16).astype(F32)
        off_row = jnp.where(lane_row == 2, off_hi, jnp.where(lane_row == 3, off - off_hi, 0.0))
        return (kpos_s[row0:row0 + n_rows, :] + off_row).astype(BF16)

    def with_feats(k_blk, row0, offset):
        return jnp.concatenate([k_blk, key_feats(row0, k_blk.shape[0], offset)], axis=1)

    diag = pl.multiple_of(i * tq, tq)
    kr = lax.broadcasted_iota(jnp.int32, (sub, 2 * sub), 0)
    qc = lax.broadcasted_iota(jnp.int32, (sub, 2 * sub), 1)
    causal = kr <= (qc & (sub - 1))
    for r in range(n_sub):
        c0 = 2 * sub * r
        lhs = with_feats(k_ref[pl.ds(diag + r * sub, sub), :], r * sub, 0.0)
        if r == 0:
            lhs = jnp.concatenate(
                [lhs, with_feats(mk_ref[...], 0, -(i * tq + N_META).astype(F32) * slope)], axis=0)
        s = _dot(lhs, qt_s[:, c0:])
        s_first = jnp.where(causal, s[:sub, :2 * sub], neg_inf)
        s_top = s_first if r == n_sub - 1 else jnp.concatenate([s_first, s[:sub, 2 * sub:]], axis=1)
        s_s[pl.ds(diag + r * sub, sub), c0:] = s_top
        m_r = jnp.max(s_top, axis=0, keepdims=True)
        if r == 0:
            sm_s[...] = s[sub:]
            m_s[...] = jnp.maximum(m_r, jnp.max(s[sub:], axis=0, keepdims=True))
        else:
            m_s[:, c0:] = jnp.maximum(m_s[:, c0:], m_r)

    def score_body(t, carry):
        start = pl.multiple_of(t * chunk, chunk)
        s = _dot(with_feats(k_ref[pl.ds(start, chunk), :], 0, (start - i * tq).astype(F32) * slope),
                 qt_s[...])
        s_s[pl.ds(start, chunk), :] = s
        m_s[...] = jnp.maximum(m_s[...], jnp.max(s, axis=0, keepdims=True))
        return carry

    dist = (SKIP_BELOW + jnp.max(q_bound - m_s[...])) * inv_slope
    n_chunks = i * (tq // chunk)
    first_key = jnp.clip(jnp.floor(i.astype(F32) * tq - dist + 1.0).astype(jnp.int32), 0, i * tq)
    first_chunk = first_key // chunk
    first_piece = (first_key - first_chunk * chunk) // piece
    in_pieces = jnp.logical_and(first_chunk < n_chunks, first_piece >= chunk // piece // 2)
    piece_lo = jnp.where(in_pieces, first_piece, chunk // piece)
    chunk_lo = first_chunk + in_pieces.astype(jnp.int32)

    def score_piece(p, carry):
        start = pl.multiple_of(first_chunk * chunk + p * piece, piece)
        s = _dot(with_feats(k_ref[pl.ds(start, piece), :], 0, (start - i * tq).astype(F32) * slope), qt_s[...])
        s_s[pl.ds(start, piece), :] = s
        m_s[...] = jnp.maximum(m_s[...], jnp.max(s, axis=0, keepdims=True))
        return carry

    lax.fori_loop(piece_lo, chunk // piece, score_piece, 0)
    lax.fori_loop(chunk_lo, n_chunks, score_body, 0)

    m = m_s[...]
    for r in range(n_sub):
        c0 = 2 * sub * r
        vt_r = vt_s[(diag + r * sub) // vtc][:, (r * sub) % vtc:(r * sub) % vtc + sub]
        p = jnp.exp2((s_s[pl.ds(diag + r * sub, sub), c0:] - m[:, c0:]).astype(BF16))
        if r == 0:
            p = jnp.concatenate([p, jnp.exp2((sm_s[...] - m).astype(BF16)),
                                 jnp.zeros((LANES - N_META, n_cols), BF16)], axis=0)
            acc_s[...] = _dot(jnp.concatenate([vt_r, mvt_s[...]], axis=1), p)
        else:
            acc_s[:, c0:] += _dot(vt_r, p)

    def pv_body(t, carry):
        start = pl.multiple_of(t * chunk, chunk)
        p = jnp.exp2((s_s[pl.ds(start, chunk), :] - m).astype(BF16))
        blk = start // vtc
        upd = _dot(vt_s[blk], p[:vtc])
        for c in range(1, chunk // vtc):
            upd = upd + _dot(vt_s[blk + c], p[c * vtc:(c + 1) * vtc])
        acc_s[...] += upd
        return carry

    def pv_piece(p, carry):
        start = pl.multiple_of(first_chunk * chunk + p * piece, piece)
        pr = jnp.exp2((s_s[pl.ds(start, piece), :] - m).astype(BF16))
        acc_s[...] += _dot(vtp_s[start // piece], pr)
        return carry

    lax.fori_loop(piece_lo, chunk // piece, pv_piece, 0)
    lax.fori_loop(chunk_lo, n_chunks, pv_body, 0)

    lam = (jnp.exp(jnp.sum(lq1_ref[...] * lk1_ref[...], axis=1, keepdims=True))
           - jnp.exp(jnp.sum(lq2_ref[...] * lk2_ref[...], axis=1, keepdims=True)) + lam_init)
    inv_l = 1.0 / acc_s[dv:dv + 1, :]
    for c in range(n_sub):
        c0 = 2 * sub * c
        o1 = acc_s[:dv, c0:c0 + sub] * inv_l[:, c0:c0 + sub]
        o2 = acc_s[:dv, c0 + sub:c0 + 2 * sub] * inv_l[:, c0 + sub:c0 + 2 * sub]
        ot = o1 - lam * o2
        yt = ot * lax.rsqrt(jnp.mean(ot * ot, axis=0, keepdims=True) + NORM_EPS)
        o_ref[c * sub:(c + 1) * sub, :] = (yt.T * hg_ref[...] * (1.0 - lam_init)).astype(o_ref.dtype)


def _diff_attention(proj, proj_t, mk, mvt, lq1, lk1, lq2, lk2, head_g, *, batch, seq, lam_init):
    n = batch * seq
    tq = ATTN_TQ
    assert seq % tq == 0 and tq % ATTN_CHUNK == 0 and ATTN_CHUNK % ATTN_VT_COLS == 0
    nq = seq // tq
    dv = DA_VDIM
    slopes = 2.0 ** (-8.0 * (np.arange(DA_HEADS, dtype=np.float64) + 1.0) / DA_HEADS) * LOG2_E
    slopes = jnp.asarray(np.concatenate([slopes, 1.0 / slopes]), F32)
    vec = lambda: pl.BlockSpec((1, DA_HEAD_DIM), lambda b, h, i: (0, 0))
    n_cols = 2 * tq
    n_vrows = dv + BF16_SUBLANES
    return pl.pallas_call(
        functools.partial(_attn_kernel, lam_init=lam_init),
        grid=(batch, DA_HEADS, nq),
        in_specs=[
            pl.BlockSpec(memory_space=pltpu.SMEM),
            pl.BlockSpec((dv, tq), lambda b, h, i: (h, b * nq + i)),
            pl.BlockSpec((seq, dv), lambda b, h, i: (b, PROJ_COLS["k_a"] * DA_HEADS + h)),
            pl.BlockSpec((dv, seq), lambda b, h, i: (DA_HEADS + h, b)),
            pl.BlockSpec((N_META, dv), lambda b, h, i: (0, h)),
            pl.BlockSpec((dv, LANES), lambda b, h, i: (h, 0)),
            vec(), vec(), vec(), vec(),
            pl.BlockSpec((1, dv), lambda b, h, i: (0, 0)),
        ],
        out_specs=pl.BlockSpec((tq, dv), lambda b, h, i: (b * nq + i, h)),
        out_shape=jax.ShapeDtypeStruct((n, DA_HEADS * dv), BF16),
        scratch_shapes=[pltpu.VMEM((n_vrows, LANES), BF16),
                        pltpu.VMEM((seq // ATTN_VT_COLS, n_vrows, ATTN_VT_COLS), BF16),
                        pltpu.VMEM((seq // ATTN_PIECE, n_vrows, ATTN_PIECE), BF16),
                        pltpu.VMEM((max(tq, ATTN_CHUNK), LANES), F32),
                        pltpu.VMEM((LANES, LANES), F32),
                        pltpu.VMEM((2 * LANES, n_cols), BF16),
                        pltpu.VMEM((seq, n_cols), F32),
                        pltpu.VMEM((N_META, n_cols), F32),
                        pltpu.VMEM((1, n_cols), F32),
                        pltpu.VMEM((n_vrows, n_cols), F32)],
        compiler_params=pltpu.CompilerParams(
            dimension_semantics=("parallel", "parallel", "arbitrary"),
            vmem_limit_bytes=V7X_VMEM_LIMIT_BYTES),
    )(slopes, proj_t, proj, proj_t, mk, mvt, lq1, lk1, lq2, lk2, head_g)


def _ret_merge_kernel(lg_ref, x_ref, yda_ref, q_ref, k_ref, v0_ref, v1_ref, gate0_ref, gate1_ref, ga_ref, gr_ref,
                      mk_ref, mv0_ref, mv1_ref, wda_ref, wret_ref, wout_ref, o_ref, state_s, dec_s,
                      *, chunk, k_scale):
    ci = pl.program_id(1)
    n_heads = state_s.shape[0]
    dk, dv = state_s.shape[1], state_s.shape[2]
    rows = lax.broadcasted_iota(jnp.int32, (chunk, 1), 0).astype(F32)
    ks = jnp.asarray(k_scale, BF16)
    v_refs, mv_refs, gate_refs = (v0_ref, v1_ref), (mv0_ref, mv1_ref), (gate0_ref, gate1_ref)
    heads_per_ref = v0_ref.shape[1] // dv

    def head_cols(refs, hh):
        return refs[hh // heads_per_ref], slice((hh % heads_per_ref) * dv, (hh % heads_per_ref + 1) * dv)

    @pl.when(ci == 0)
    def _():
        ri = lax.broadcasted_iota(jnp.int32, (chunk, chunk), 0)
        cc = lax.broadcasted_iota(jnp.int32, (chunk, chunk), 1)
        diff = (ri - cc).astype(F32)
        for hh in range(n_heads):
            lg = lg_ref[hh]
            dec_s[hh] = jnp.where(diff >= 0, jnp.exp(jnp.maximum(diff, 0.0) * lg), 0.0)
            kd = ((mk_ref[:, hh * dk:(hh + 1) * dk] * ks).astype(F32) * jnp.exp((chunk - 1.0 - rows) * lg))
            mv_ref, cols = head_cols(mv_refs, hh)
            state_s[hh] = _dot_tn(kd.astype(BF16), mv_ref[:, cols])

    y_heads = []
    for hh in range(n_heads):
        lg = lg_ref[hh]
        end_dec = jnp.exp((chunk - 1.0 - rows) * lg)
        cross_dec = jnp.exp((rows + 1.0) * lg)
        chunk_dec = jnp.exp(jnp.full((1, 1), float(chunk), F32) * lg)
        v_ref, vcols = head_cols(v_refs, hh)
        gate_ref, gcols = head_cols(gate_refs, hh)
        state = state_s[hh]
        y_chunks = []
        for r0 in range(0, x_ref.shape[0], chunk):
            rs = slice(r0, r0 + chunk)
            qc = q_ref[rs, hh * dk:(hh + 1) * dk]
            kc = k_ref[rs, hh * dk:(hh + 1) * dk] * ks
            vc = v_ref[rs, vcols]
            a = _dot_nt(qc, kc) * dec_s[hh]
            o = _dot(a.astype(BF16), vc) + _dot(qc, state.astype(BF16)) * cross_dec
            kd = (kc.astype(F32) * end_dec).astype(BF16)
            state = state * chunk_dec + _dot_tn(kd, vc)
            g = gate_ref[rs, gcols].astype(F32)
            y_chunks.append((_rms(o) * (g * jax.nn.sigmoid(g))).astype(BF16))
        state_s[hh] = state
        y_heads.append(jnp.concatenate(y_chunks, axis=0))
    y_ret = jnp.concatenate(y_heads, axis=1)

    a = _dot(yda_ref[...], wda_ref[...])
    r = _dot(y_ret, wret_ref[...])
    merged = (jax.nn.sigmoid(ga_ref[...].astype(F32)) * a
              + jax.nn.sigmoid(gr_ref[...].astype(F32)) * r)
    o_ref[...] = x_ref[...] + _dot(merged.astype(BF16), wout_ref[...])


def _ret_merge(x2, y_da, proj, mk, mv, w_da, w_ret, w_out, *, batch, seq, chunk, step_rows):
    n, d = x2.shape
    nc = seq // step_rows
    dk = d // RET_HEADS
    dv = 2 * d // RET_HEADS
    log_gamma = jnp.asarray(np.log(1.0 - 2.0 ** (-5.0 - np.arange(RET_HEADS, dtype=np.float32))), F32)
    row_tile = lambda width, col: pl.BlockSpec((step_rows, width), lambda b, c: (b * nc + c, col))
    col = lambda name, off=0: row_tile(d, PROJ_COLS[name] + off)
    meta = lambda off: pl.BlockSpec((chunk, d), lambda b, c: (0, off))
    const = lambda shape: pl.BlockSpec(shape, lambda b, c: (0, 0), pipeline_mode=pl.Buffered(1))
    return pl.pallas_call(
        functools.partial(_ret_merge_kernel, chunk=chunk, k_scale=dk ** -0.5),
        grid=(batch, nc),
        in_specs=[
            pl.BlockSpec(memory_space=pltpu.SMEM),
            row_tile(d, 0),
            row_tile(d, 0),
            col("q_r"), col("k_r"), col("v_r"), col("v_r", 1), col("gate_r"), col("gate_r", 1),
            col("g_a"), col("g_r"),
            meta(0), meta(0), meta(1),
            const(w_da.shape), const(w_ret.shape), const(w_out.shape),
        ],
        out_specs=row_tile(d, 0),
        out_shape=jax.ShapeDtypeStruct((n, d), F32),
        scratch_shapes=[pltpu.VMEM((RET_HEADS, dk, dv), F32), pltpu.VMEM((RET_HEADS, chunk, chunk), F32)],
        compiler_params=pltpu.CompilerParams(
            dimension_semantics=("parallel", "arbitrary"),
            vmem_limit_bytes=V7X_VMEM_LIMIT_BYTES),
    )(log_gamma, x2, y_da, proj, proj, proj, proj, proj, proj, proj, proj, mk, mv, mv, w_da, w_ret, w_out)


def _mlp_kernel(h_ref, g1_ref, wup_ref, wdown_ref, g2_ref, o_ref, *, tf):
    hres = h_ref[...]
    u = (_rms(hres) * g1_ref[...]).astype(BF16)
    acc = hres
    for f in range(0, wup_ref.shape[1], tf):
        a = jnp.maximum(_dot(u, wup_ref[:, f:f + tf]), 0.0)
        acc = acc + _dot((a * a).astype(BF16), wdown_ref[f:f + tf, :])
    o_ref[...] = _rms(acc) * g2_ref[...]


def _mlp(h1, g1, w_up, w_down, g2, *, tm, tf):
    n, d = h1.shape
    const = lambda shape: pl.BlockSpec(shape, lambda i: (0, 0), pipeline_mode=pl.Buffered(1))
    return pl.pallas_call(
        functools.partial(_mlp_kernel, tf=tf),
        grid=(n // tm,),
        in_specs=[
            pl.BlockSpec((tm, d), lambda i: (i, 0)),
            pl.BlockSpec((1, d), lambda i: (0, 0)),
            const(w_up.shape), const(w_down.shape),
            pl.BlockSpec((1, d), lambda i: (0, 0)),
        ],
        out_specs=pl.BlockSpec((tm, d), lambda i: (i, 0)),
        out_shape=jax.ShapeDtypeStruct((n, d), F32),
        compiler_params=pltpu.CompilerParams(
            dimension_semantics=("parallel",),
            vmem_limit_bytes=V7X_VMEM_LIMIT_BYTES),
    )(h1, g1, w_up, w_down, g2)


def _tile(n, target):
    t = min(n, target)
    assert n % t == 0, (n, t)
    return t


def kernel(x, meta_tokens, norm_mix_g, w_in, da_lambda_q1, da_lambda_k1, da_lambda_q2, da_lambda_k2,
           da_head_g, w_proj_da, w_proj_ret, w_out, norm_mlp_g, w_up, w_down, final_norm_g):
    batch, seq, d = x.shape
    assert w_in.shape[0] == 1, "single-layer block"
    assert d // RET_HEADS == 2 * LANES and DA_HEADS * DA_VDIM == d
    n = batch * seq
    layer = 0
    lam_init = 0.8 - 0.6 * math.exp(-0.3 * layer)

    x2 = x.reshape(n, d)
    w_in_b = w_in[layer].astype(BF16)
    g_mix = norm_mix_g[layer].reshape(1, d)

    w_rows = jnp.concatenate([w_in_b[:, d:2 * d], w_in_b[:, 3 * d:]], axis=1)
    w_feat = jnp.concatenate([w_in_b[:, :d], w_in_b[:, 2 * d:3 * d]], axis=1).T
    proj, proj_t = _inproj_t(x2, g_mix, w_rows, w_feat, tm=_tile(n, 512), col_group=_tile(w_rows.shape[1], 2304),
                             n_scaled=d, scale=DA_HEAD_DIM ** -0.5 * LOG2_E)
    proj_m = _inproj(meta_tokens.astype(x.dtype), g_mix, w_in_b, tm=N_META, tn=_tile(w_in_b.shape[1], 1024))

    mvt_a = jnp.pad(proj_m[:, 2 * d:3 * d].T, ((0, 0), (0, LANES - N_META)))
    row = lambda v: v[layer].reshape(1, -1).astype(F32)
    y_da = _diff_attention(proj, proj_t, proj_m[:, d:2 * d], mvt_a, row(da_lambda_q1), row(da_lambda_k1),
                           row(da_lambda_q2), row(da_lambda_k2), row(da_head_g), batch=batch, seq=seq,
                           lam_init=lam_init)

    chunk = _tile(seq, 256)
    pad_r = ((chunk - N_META, 0), (0, 0))
    mk_r = jnp.pad(proj_m[:, 4 * d:5 * d], pad_r)
    mv_r = jnp.pad(proj_m[:, 5 * d:7 * d], pad_r)
    h1 = _ret_merge(x2, y_da, proj, mk_r, mv_r, w_proj_da[layer].astype(BF16), w_proj_ret[layer].astype(BF16),
                    w_out[layer].astype(BF16), batch=batch, seq=seq, chunk=chunk, step_rows=_tile(seq, 512))
    out = _mlp(h1, norm_mlp_g[layer].reshape(1, d), w_up[layer].astype(BF16), w_down[layer].astype(BF16),
               final_norm_g.reshape(1, d), tm=_tile(n, 1024), tf=1024)
    return out.reshape(batch, seq, d)
```

```python
import functools
import math

import numpy as np
import jax
import jax.numpy as jnp
from jax import lax
from jax.experimental import pallas as pl
from jax.experimental.pallas import tpu as pltpu

N_META = 16
DA_HEADS = 8
DA_HEAD_DIM = 64
DA_VDIM = 2 * DA_HEAD_DIM
RET_HEADS = 4
NORM_EPS = 1e-6

F32 = jnp.float32
BF16 = jnp.bfloat16

V7X_VMEM_LIMIT_BYTES = 56 * 1024 * 1024
LANES = 128
BF16_SUBLANES = 16


def _dot(a, b):
    return jnp.dot(a, b, preferred_element_type=F32)


def _dot_nt(a, b):
    return lax.dot_general(a, b, (((1,), (1,)), ((), ())), preferred_element_type=F32)


def _dot_tn(a, b):
    return lax.dot_general(a, b, (((0,), (0,)), ((), ())), preferred_element_type=F32)


def _rms(xf):
    return xf * lax.rsqrt(jnp.mean(xf * xf, axis=-1, keepdims=True) + NORM_EPS)


PROJ_COLS = {"k_a": 0, "q_r": 1, "k_r": 2, "v_r": 3, "gate_r": 5, "g_a": 7, "g_r": 8}


def _inproj_kernel(x_ref, g_ref, w_ref, o_ref, u_ref):
    @pl.when(pl.program_id(1) == 0)
    def _():
        u_ref[...] = (_rms(x_ref[...]) * g_ref[...]).astype(BF16)

    o_ref[...] = _dot(u_ref[...], w_ref[...]).astype(o_ref.dtype)


def _inproj_t_kernel(x_ref, g_ref, w_ref, wt_ref, o_ref, ot_ref, *, n_scaled, scale, col_group):
    u = (_rms(x_ref[...]) * g_ref[...]).astype(BF16)
    ot = _dot_nt(wt_ref[...], u)
    ot_ref[:n_scaled, :] = (ot[:n_scaled] * scale).astype(ot_ref.dtype)
    ot_ref[n_scaled:, :] = ot[n_scaled:].astype(ot_ref.dtype)
    for c0 in range(0, w_ref.shape[1], col_group):
        o_ref[:, c0:c0 + col_group] = _dot(u, w_ref[:, c0:c0 + col_group]).astype(o_ref.dtype)


def _inproj(x2, g, w, *, tm, tn):
    n, d = x2.shape
    d_in = w.shape[1]
    return pl.pallas_call(
        _inproj_kernel,
        grid=(n // tm, d_in // tn),
        in_specs=[
            pl.BlockSpec((tm, d), lambda i, j: (i, 0)),
            pl.BlockSpec((1, d), lambda i, j: (0, 0)),
            pl.BlockSpec((d, tn), lambda i, j: (0, j)),
        ],
        out_specs=pl.BlockSpec((tm, tn), lambda i, j: (i, j)),
        out_shape=jax.ShapeDtypeStruct((n, d_in), BF16),
        scratch_shapes=[pltpu.VMEM((tm, d), BF16)],
        compiler_params=pltpu.CompilerParams(
            dimension_semantics=("parallel", "arbitrary"),
            vmem_limit_bytes=V7X_VMEM_LIMIT_BYTES),
    )(x2, g, w)


def _inproj_t(x2, g, w, wt, *, tm, col_group, n_scaled, scale):
    n, d = x2.shape
    d_in = w.shape[1]
    d_t = wt.shape[0]
    const = lambda shape: pl.BlockSpec(shape, lambda i: (0, 0), pipeline_mode=pl.Buffered(1))
    return pl.pallas_call(
        functools.partial(_inproj_t_kernel, n_scaled=n_scaled, scale=scale, col_group=col_group),
        grid=(n // tm,),
        in_specs=[
            pl.BlockSpec((tm, d), lambda i: (i, 0)),
            pl.BlockSpec((1, d), lambda i: (0, 0)),
            const(w.shape), const(wt.shape),
        ],
        out_specs=[pl.BlockSpec((tm, d_in), lambda i: (i, 0)),
                   pl.BlockSpec((d_t, tm), lambda i: (0, i))],
        out_shape=[jax.ShapeDtypeStruct((n, d_in), BF16), jax.ShapeDtypeStruct((d_t, n), BF16)],
        compiler_params=pltpu.CompilerParams(
            dimension_semantics=("parallel",),
            vmem_limit_bytes=V7X_VMEM_LIMIT_BYTES),
    )(x2, g, w, wt)


ATTN_TQ = 1024
ATTN_SUB = 256
ATTN_CHUNK = 1024
ATTN_VT_COLS = 512
ATTN_PIECE = 256
N_FEAT = 4
LOG2_E = math.log2(math.e)
SKIP_BELOW = 140.0


def _attn_block(i, slopes_ref, qt_ref, k_ref, vt_ref, mk_ref, mvt_ref, lq1_ref, lk1_ref, lq2_ref, lk2_ref,
                 hg_ref, o_ref, mvt_s, vt_s, vtp_s, kpos_s, kabs_s, qt_s, s_s, sm_s, m_s, acc_s, *, lam_init):
    h = pl.program_id(1)
    slope = slopes_ref[h]
    inv_slope = slopes_ref[DA_HEADS + h]
    tq, sub, chunk, vtc, piece = ATTN_TQ, ATTN_SUB, ATTN_CHUNK, ATTN_VT_COLS, ATTN_PIECE
    q0 = i * tq
    n_sub = tq // sub
    dv = vt_ref.shape[0]
    n_cols = 2 * tq
    neg_inf = jnp.float32(-jnp.inf)
    lane_row = lax.broadcasted_iota(jnp.int32, (1, LANES), 1)

    def prepare():
        n_ones = vt_s.shape[1] - dv
        for c in range(vt_s.shape[0]):
            vt_s[c, :dv, :] = vt_ref[:, c * vtc:(c + 1) * vtc]
            vt_s[c, dv:, :] = jnp.ones((n_ones, vtc), vt_s.dtype)
        for c in range(vtp_s.shape[0]):
            vtp_s[c, :dv, :] = vt_ref[:, c * piece:(c + 1) * piece]
            vtp_s[c, dv:, :] = jnp.ones((n_ones, piece), vtp_s.dtype)
        mvt_s[:dv, :] = mvt_ref[...]
        mvt_s[dv:, :] = jnp.ones((n_ones, LANES), mvt_s.dtype)
        pos = lax.broadcasted_iota(jnp.int32, kpos_s.shape, 0).astype(F32) * slope
        plane = lax.broadcasted_iota(jnp.int32, kpos_s.shape, 1)
        pos_hi = pos.astype(BF16).astype(F32)
        kpos_s[...] = jnp.where(plane == 0, pos_hi, jnp.where(plane == 1, pos - pos_hi, 0.0))
        frow = lax.broadcasted_iota(jnp.int32, (LANES, n_cols), 0)
        qt_s[:LANES, :] = jnp.zeros((LANES, n_cols), qt_s.dtype)
        qt_s[LANES:, :] = jnp.where(frow < N_FEAT, 1.0, 0.0).astype(qt_s.dtype)
        kabs = jnp.max(jnp.abs(k_ref[...].astype(F32)), axis=0, keepdims=True)
        kabs_s[...] = jnp.broadcast_to(kabs, kabs_s.shape).T

    if i == 0:
        prepare()

    for c in range(n_sub):
        qt_s[:DA_HEAD_DIM, 2 * c * sub:(2 * c + 1) * sub] = qt_ref[:DA_HEAD_DIM, q0 + c * sub:q0 + (c + 1) * sub]
        qt_s[DA_HEAD_DIM:LANES, (2 * c + 1) * sub:(2 * c + 2) * sub] = qt_ref[DA_HEAD_DIM:, q0 + c * sub:q0 + (c + 1) * sub]

    q_prod = jnp.abs(qt_ref[:, q0:q0 + tq].astype(F32)) * jnp.tile(kabs_s[...], (1, tq // LANES))
    b1 = jnp.sum(q_prod[:DA_HEAD_DIM], axis=0, keepdims=True)
    b2 = jnp.sum(q_prod[DA_HEAD_DIM:], axis=0, keepdims=True)
    q_bound = jnp.concatenate(
        [b[:, c * sub:(c + 1) * sub] for c in range(n_sub) for b in (b1, b2)], axis=1)

    def key_feats(row0, n_rows, offset):
        off = jnp.full((1, LANES), offset, F32)
        off_hi = off.astype(BF16).astype(F32)
        off_row = jnp.where(lane_row == 2, off_hi, jnp.where(lane_row == 3, off - off_hi, 0.0))
        return (kpos_s[row0:row0 + n_rows, :] + off_row).astype(BF16)

    def with_feats(k_blk, row0, offset):
        return jnp.concatenate([k_blk, key_feats(row0, k_blk.shape[0], offset)], axis=1)

    diag = q0
    kr = lax.broadcasted_iota(jnp.int32, (sub, 2 * sub), 0)
    qc = lax.broadcasted_iota(jnp.int32, (sub, 2 * sub), 1)
    causal = kr <= (qc & (sub - 1))
    for r in range(n_sub):
        c0 = 2 * sub * r
        lhs = with_feats(k_ref[pl.ds(diag + r * sub, sub), :], r * sub, 0.0)
        if r == 0:
            lhs = jnp.concatenate(
                [lhs, with_feats(mk_ref[...], 0, -float(q0 + N_META) * slope)], axis=0)
        s = _dot(lhs, qt_s[:, c0:])
        s_first = jnp.where(causal, s[:sub, :2 * sub], neg_inf)
        s_top = s_first if r == n_sub - 1 else jnp.concatenate([s_first, s[:sub, 2 * sub:]], axis=1)
        s_s[pl.ds(diag + r * sub, sub), c0:] = s_top
        m_r = jnp.max(s_top, axis=0, keepdims=True)
        if r == 0:
            sm_s[...] = s[sub:]
            m_s[...] = jnp.maximum(m_r, jnp.max(s[sub:], axis=0, keepdims=True))
        else:
            m_s[:, c0:] = jnp.maximum(m_s[:, c0:], m_r)

    def score_body(t, carry):
        start = pl.multiple_of(t * chunk, chunk)
        s = _dot(with_feats(k_ref[pl.ds(start, chunk), :], 0, (start - q0).astype(F32) * slope),
                 qt_s[...])
        s_s[pl.ds(start, chunk), :] = s
        m_s[...] = jnp.maximum(m_s[...], jnp.max(s, axis=0, keepdims=True))
        return carry

    dist = (SKIP_BELOW + jnp.max(q_bound - m_s[...])) * inv_slope
    n_chunks = i * (tq // chunk)
    first_key = jnp.clip(jnp.floor(float(q0) - dist + 1.0).astype(jnp.int32), 0, q0)
    first_chunk = first_key // chunk
    first_piece = (first_key - first_chunk * chunk) // piece
    in_pieces = jnp.logical_and(first_chunk < n_chunks, first_piece >= chunk // piece // 2)
    piece_lo = jnp.where(in_pieces, first_piece, chunk // piece)
    chunk_lo = first_chunk + in_pieces.astype(jnp.int32)

    def score_piece(p, carry):
        start = pl.multiple_of(first_chunk * chunk + p * piece, piece)
        s = _dot(with_feats(k_ref[pl.ds(start, piece), :], 0, (start - q0).astype(F32) * slope), qt_s[...])
        s_s[pl.ds(start, piece), :] = s
        m_s[...] = jnp.maximum(m_s[...], jnp.max(s, axis=0, keepdims=True))
        return carry

    lax.fori_loop(piece_lo, chunk // piece, score_piece, 0)
    lax.fori_loop(chunk_lo, n_chunks, score_body, 0)

    m = m_s[...]
    for r in range(n_sub):
        c0 = 2 * sub * r
        vt_r = vt_s[(diag + r * sub) // vtc][:, (r * sub) % vtc:(r * sub) % vtc + sub]
        p = jnp.exp2((s_s[pl.ds(diag + r * sub, sub), c0:] - m[:, c0:]).astype(BF16))
        if r == 0:
            p = jnp.concatenate([p, jnp.exp2((sm_s[...] - m).astype(BF16)),
                                 jnp.zeros((LANES - N_META, n_cols), BF16)], axis=0)
            acc_s[...] = _dot(jnp.concatenate([vt_r, mvt_s[...]], axis=1), p)
        else:
            acc_s[:, c0:] += _dot(vt_r, p)

    def pv_body(t, carry):
        start = pl.multiple_of(t * chunk, chunk)
        p = jnp.exp2((s_s[pl.ds(start, chunk), :] - m).astype(BF16))
        blk = start // vtc
        upd = _dot(vt_s[blk], p[:vtc])
        for c in range(1, chunk // vtc):
            upd = upd + _dot(vt_s[blk + c], p[c * vtc:(c + 1) * vtc])
        acc_s[...] += upd
        return carry

    def pv_piece(p, carry):
        start = pl.multiple_of(first_chunk * chunk + p * piece, piece)
        pr = jnp.exp2((s_s[pl.ds(start, piece), :] - m).astype(BF16))
        acc_s[...] += _dot(vtp_s[start // piece], pr)
        return carry

    lax.fori_loop(piece_lo, chunk // piece, pv_piece, 0)
    lax.fori_loop(chunk_lo, n_chunks, pv_body, 0)

    lam = (jnp.exp(jnp.sum(lq1_ref[...] * lk1_ref[...], axis=1, keepdims=True))
           - jnp.exp(jnp.sum(lq2_ref[...] * lk2_ref[...], axis=1, keepdims=True)) + lam_init)
    inv_l = 1.0 / acc_s[dv:dv + 1, :]
    for c in range(n_sub):
        c0 = 2 * sub * c
        o1 = acc_s[:dv, c0:c0 + sub] * inv_l[:, c0:c0 + sub]
        o2 = acc_s[:dv, c0 + sub:c0 + 2 * sub] * inv_l[:, c0 + sub:c0 + 2 * sub]
        ot = o1 - lam * o2
        yt = ot * lax.rsqrt(jnp.mean(ot * ot, axis=0, keepdims=True) + NORM_EPS)
        o_ref[q0 + c * sub:q0 + (c + 1) * sub, :] = (yt.T * hg_ref[...] * (1.0 - lam_init)).astype(o_ref.dtype)


def _attn_kernel(*refs, lam_init):
    qt_ref = refs[1]
    for i in range(qt_ref.shape[1] // ATTN_TQ):
        _attn_block(i, *refs, lam_init=lam_init)


def _diff_attention(proj, proj_t, mk, mvt, lq1, lk1, lq2, lk2, head_g, *, batch, seq, lam_init):
    n = batch * seq
    tq = ATTN_TQ
    assert seq % tq == 0 and tq % ATTN_CHUNK == 0 and ATTN_CHUNK % ATTN_VT_COLS == 0
    dv = DA_VDIM
    slopes = 2.0 ** (-8.0 * (np.arange(DA_HEADS, dtype=np.float64) + 1.0) / DA_HEADS) * LOG2_E
    slopes = jnp.asarray(np.concatenate([slopes, 1.0 / slopes]), F32)
    vec = lambda: pl.BlockSpec((1, DA_HEAD_DIM), lambda b, h: (0, 0))
    n_cols = 2 * tq
    n_vrows = dv + BF16_SUBLANES
    return pl.pallas_call(
        functools.partial(_attn_kernel, lam_init=lam_init),
        grid=(batch, DA_HEADS),
        in_specs=[
            pl.BlockSpec(memory_space=pltpu.SMEM),
            pl.BlockSpec((dv, seq), lambda b, h: (h, b)),
            pl.BlockSpec((seq, dv), lambda b, h: (b, PROJ_COLS["k_a"] * DA_HEADS + h)),
            pl.BlockSpec((dv, seq), lambda b, h: (DA_HEADS + h, b)),
            pl.BlockSpec((N_META, dv), lambda b, h: (0, h)),
            pl.BlockSpec((dv, LANES), lambda b, h: (h, 0)),
            vec(), vec(), vec(), vec(),
            pl.BlockSpec((1, dv), lambda b, h: (0, 0)),
        ],
        out_specs=pl.BlockSpec((seq, dv), lambda b, h: (b, h)),
        out_shape=jax.ShapeDtypeStruct((n, DA_HEADS * dv), BF16),
        scratch_shapes=[pltpu.VMEM((n_vrows, LANES), BF16),
                        pltpu.VMEM((seq // ATTN_VT_COLS, n_vrows, ATTN_VT_COLS), BF16),
                        pltpu.VMEM((seq // ATTN_PIECE, n_vrows, ATTN_PIECE), BF16),
                        pltpu.VMEM((max(tq, ATTN_CHUNK), LANES), F32),
                        pltpu.VMEM((LANES, LANES), F32),
                        pltpu.VMEM((2 * LANES, n_cols), BF16),
                        pltpu.VMEM((seq, n_cols), F32),
                        pltpu.VMEM((N_META, n_cols), F32),
                        pltpu.VMEM((1, n_cols), F32),
                        pltpu.VMEM((n_vrows, n_cols), F32)],
        compiler_params=pltpu.CompilerParams(
            dimension_semantics=("parallel", "parallel"),
            vmem_limit_bytes=V7X_VMEM_LIMIT_BYTES),
    )(slopes, proj_t, proj, proj_t, mk, mvt, lq1, lk1, lq2, lk2, head_g)


def _ret_merge_kernel(lg_ref, x_ref, yda_ref, q_ref, k_ref, v0_ref, v1_ref, gate0_ref, gate1_ref, ga_ref, gr_ref,
                      mk_ref, mv0_ref, mv1_ref, wda_ref, wret_ref, wout_ref, o_ref, state_s, dec_s,
                      *, chunk, k_scale):
    ci = pl.program_id(1)
    n_heads = state_s.shape[0]
    dk, dv = state_s.shape[1], state_s.shape[2]
    rows = lax.broadcasted_iota(jnp.int32, (chunk, 1), 0).astype(F32)
    ks = jnp.asarray(k_scale, BF16)
    v_refs, mv_refs, gate_refs = (v0_ref, v1_ref), (mv0_ref, mv1_ref), (gate0_ref, gate1_ref)
    heads_per_ref = v0_ref.shape[1] // dv

    def head_cols(refs, hh):
        return refs[hh // heads_per_ref], slice((hh % heads_per_ref) * dv, (hh % heads_per_ref + 1) * dv)

    @pl.when(ci == 0)
    def _():
        ri = lax.broadcasted_iota(jnp.int32, (chunk, chunk), 0)
        cc = lax.broadcasted_iota(jnp.int32, (chunk, chunk), 1)
        diff = (ri - cc).astype(F32)
        for hh in range(n_heads):
            lg = lg_ref[hh]
            dec_s[hh] = jnp.where(diff >= 0, jnp.exp(jnp.maximum(diff, 0.0) * lg), 0.0)
            kd = ((mk_ref[:, hh * dk:(hh + 1) * dk] * ks).astype(F32) * jnp.exp((chunk - 1.0 - rows) * lg))
            mv_ref, cols = head_cols(mv_refs, hh)
            state_s[hh] = _dot_tn(kd.astype(BF16), mv_ref[:, cols])

    y_heads = []
    for hh in range(n_heads):
        lg = lg_ref[hh]
        end_dec = jnp.exp((chunk - 1.0 - rows) * lg)
        cross_dec = jnp.exp((rows + 1.0) * lg)
        chunk_dec = jnp.exp(jnp.full((1, 1), float(chunk), F32) * lg)
        v_ref, vcols = head_cols(v_refs, hh)
        gate_ref, gcols = head_cols(gate_refs, hh)
        state = state_s[hh]
        y_chunks = []
        for r0 in range(0, x_ref.shape[0], chunk):
            rs = slice(r0, r0 + chunk)
            qc = q_ref[rs, hh * dk:(hh + 1) * dk]
            kc = k_ref[rs, hh * dk:(hh + 1) * dk] * ks
            vc = v_ref[rs, vcols]
            a = _dot_nt(qc, kc) * dec_s[hh]
            o = _dot(a.astype(BF16), vc) + _dot(qc, state.astype(BF16)) * cross_dec
            kd = (kc.astype(F32) * end_dec).astype(BF16)
            state = state * chunk_dec + _dot_tn(kd, vc)
            g = gate_ref[rs, gcols].astype(F32)
            y_chunks.append((_rms(o) * (g * jax.nn.sigmoid(g))).astype(BF16))
        state_s[hh] = state
        y_heads.append(jnp.concatenate(y_chunks, axis=0))
    y_ret = jnp.concatenate(y_heads, axis=1)

    a = _dot(yda_ref[...], wda_ref[...])
    r = _dot(y_ret, wret_ref[...])
    merged = (jax.nn.sigmoid(ga_ref[...].astype(F32)) * a
              + jax.nn.sigmoid(gr_ref[...].astype(F32)) * r)
    o_ref[...] = x_ref[...] + _dot(merged.astype(BF16), wout_ref[...])


def _ret_merge(x2, y_da, proj, mk, mv, w_da, w_ret, w_out, *, batch, seq, chunk, step_rows):
    n, d = x2.shape
    nc = seq // step_rows
    dk = d // RET_HEADS
    dv = 2 * d // RET_HEADS
    log_gamma = jnp.asarray(np.log(1.0 - 2.0 ** (-5.0 - np.arange(RET_HEADS, dtype=np.float32))), F32)
    row_tile = lambda width, col: pl.BlockSpec((step_rows, width), lambda b, c: (b * nc + c, col))
    col = lambda name, off=0: row_tile(d, PROJ_COLS[name] + off)
    meta = lambda off: pl.BlockSpec((chunk, d), lambda b, c: (0, off))
    const = lambda shape: pl.BlockSpec(shape, lambda b, c: (0, 0), pipeline_mode=pl.Buffered(1))
    return pl.pallas_call(
        functools.partial(_ret_merge_kernel, chunk=chunk, k_scale=dk ** -0.5),
        grid=(batch, nc),
        in_specs=[
            pl.BlockSpec(memory_space=pltpu.SMEM),
            row_tile(d, 0),
            row_tile(d, 0),
            col("q_r"), col("k_r"), col("v_r"), col("v_r", 1), col("gate_r"), col("gate_r", 1),
            col("g_a"), col("g_r"),
            meta(0), meta(0), meta(1),
            const(w_da.shape), const(w_ret.shape), const(w_out.shape),
        ],
        out_specs=row_tile(d, 0),
        out_shape=jax.ShapeDtypeStruct((n, d), F32),
        scratch_shapes=[pltpu.VMEM((RET_HEADS, dk, dv), F32), pltpu.VMEM((RET_HEADS, chunk, chunk), F32)],
        compiler_params=pltpu.CompilerParams(
            dimension_semantics=("parallel", "arbitrary"),
            vmem_limit_bytes=V7X_VMEM_LIMIT_BYTES),
    )(log_gamma, x2, y_da, proj, proj, proj, proj, proj, proj, proj, proj, mk, mv, mv, w_da, w_ret, w_out)


def _mlp_kernel(h_ref, g1_ref, wup_ref, wdown_ref, g2_ref, o_ref, *, tf):
    hres = h_ref[...]
    u = (_rms(hres) * g1_ref[...]).astype(BF16)
    acc = hres
    for f in range(0, wup_ref.shape[1], tf):
        a = jnp.maximum(_dot(u, wup_ref[:, f:f + tf]), 0.0)
        acc = acc + _dot((a * a).astype(BF16), wdown_ref[f:f + tf, :])
    o_ref[...] = _rms(acc) * g2_ref[...]


def _mlp(h1, g1, w_up, w_down, g2, *, tm, tf):
    n, d = h1.shape
    const = lambda shape: pl.BlockSpec(shape, lambda i: (0, 0), pipeline_mode=pl.Buffered(1))
    return pl.pallas_call(
        functools.partial(_mlp_kernel, tf=tf),
        grid=(n // tm,),
        in_specs=[
            pl.BlockSpec((tm, d), lambda i: (i, 0)),
            pl.BlockSpec((1, d), lambda i: (0, 0)),
            const(w_up.shape), const(w_down.shape),
            pl.BlockSpec((1, d), lambda i: (0, 0)),
        ],
        out_specs=pl.BlockSpec((tm, d), lambda i: (i, 0)),
        out_shape=jax.ShapeDtypeStruct((n, d), F32),
        compiler_params=pltpu.CompilerParams(
            dimension_semantics=("parallel",),
            vmem_limit_bytes=V7X_VMEM_LIMIT_BYTES),
    )(h1, g1, w_up, w_down, g2)


def _tile(n, target):
    t = min(n, target)
    assert n % t == 0, (n, t)
    return t


def kernel(x, meta_tokens, norm_mix_g, w_in, da_lambda_q1, da_lambda_k1, da_lambda_q2, da_lambda_k2,
           da_head_g, w_proj_da, w_proj_ret, w_out, norm_mlp_g, w_up, w_down, final_norm_g):
    batch, seq, d = x.shape
    assert w_in.shape[0] == 1, "single-layer block"
    assert d // RET_HEADS == 2 * LANES and DA_HEADS * DA_VDIM == d
    n = batch * seq
    layer = 0
    lam_init = 0.8 - 0.6 * math.exp(-0.3 * layer)

    x2 = x.reshape(n, d)
    w_in_b = w_in[layer].astype(BF16)
    g_mix = norm_mix_g[layer].reshape(1, d)

    w_rows = jnp.concatenate([w_in_b[:, d:2 * d], w_in_b[:, 3 * d:]], axis=1)
    w_feat = jnp.concatenate([w_in_b[:, :d], w_in_b[:, 2 * d:3 * d]], axis=1).T
    proj, proj_t = _inproj_t(x2, g_mix, w_rows, w_feat, tm=_tile(n, 512), col_group=_tile(w_rows.shape[1], 2304),
                             n_scaled=d, scale=DA_HEAD_DIM ** -0.5 * LOG2_E)
    proj_m = _inproj(meta_tokens.astype(x.dtype), g_mix, w_in_b, tm=N_META, tn=_tile(w_in_b.shape[1], 1024))

    mvt_a = jnp.pad(proj_m[:, 2 * d:3 * d].T, ((0, 0), (0, LANES - N_META)))
    row = lambda v: v[layer].reshape(1, -1).astype(F32)
    y_da = _diff_attention(proj, proj_t, proj_m[:, d:2 * d], mvt_a, row(da_lambda_q1), row(da_lambda_k1),
                           row(da_lambda_q2), row(da_lambda_k2), row(da_head_g), batch=batch, seq=seq,
                           lam_init=lam_init)

    chunk = _tile(seq, 256)
    pad_r = ((chunk - N_META, 0), (0, 0))
    mk_r = jnp.pad(proj_m[:, 4 * d:5 * d], pad_r)
    mv_r = jnp.pad(proj_m[:, 5 * d:7 * d], pad_r)
    h1 = _ret_merge(x2, y_da, proj, mk_r, mv_r, w_proj_da[layer].astype(BF16), w_proj_ret[layer].astype(BF16),
                    w_out[layer].astype(BF16), batch=batch, seq=seq, chunk=chunk, step_rows=_tile(seq, 512))
    out = _mlp(h1, norm_mlp_g[layer].reshape(1, d), w_up[layer].astype(BF16), w_down[layer].astype(BF16),
               final_norm_g.reshape(1, d), tm=_tile(n, 1024), tf=1024)
    return out.reshape(batch, seq, d)
```

```python
import functools
import math

import numpy as np
import jax
import jax.numpy as jnp
from jax import lax
from jax.experimental import pallas as pl
from jax.experimental.pallas import tpu as pltpu

N_META = 16
DA_HEADS = 8
DA_HEAD_DIM = 64
DA_VDIM = 2 * DA_HEAD_DIM
RET_HEADS = 4
NORM_EPS = 1e-6

F32 = jnp.float32
BF16 = jnp.bfloat16

V7X_VMEM_LIMIT_BYTES = 56 * 1024 * 1024
LANES = 128
BF16_SUBLANES = 16


def _dot(a, b):
    return jnp.dot(a, b, preferred_element_type=F32)


def _dot_nt(a, b):
    return lax.dot_general(a, b, (((1,), (1,)), ((), ())), preferred_element_type=F32)


def _dot_tn(a, b):
    return lax.dot_general(a, b, (((0,), (0,)), ((), ())), preferred_element_type=F32)


def _rms(xf):
    return xf * lax.rsqrt(jnp.mean(xf * xf, axis=-1, keepdims=True) + NORM_EPS)


PROJ_COLS = {"k_a": 0, "q_r": 1, "k_r": 2, "v_r": 3, "gate_r": 5, "g_a": 7, "g_r": 8}


def _inproj_kernel(x_ref, g_ref, w_ref, o_ref, u_ref):
    @pl.when(pl.program_id(1) == 0)
    def _():
        u_ref[...] = (_rms(x_ref[...]) * g_ref[...]).astype(BF16)

    o_ref[...] = _dot(u_ref[...], w_ref[...]).astype(o_ref.dtype)


def _inproj_t_kernel(x_ref, g_ref, w_ref, wt_ref, o_ref, ot_ref, *, n_scaled, scale, col_group):
    u = (_rms(x_ref[...]) * g_ref[...]).astype(BF16)
    ot = _dot_nt(wt_ref[...], u)
    ot_ref[:n_scaled, :] = (ot[:n_scaled] * scale).astype(ot_ref.dtype)
    ot_ref[n_scaled:, :] = ot[n_scaled:].astype(ot_ref.dtype)
    for c0 in range(0, w_ref.shape[1], col_group):
        o_ref[:, c0:c0 + col_group] = _dot(u, w_ref[:, c0:c0 + col_group]).astype(o_ref.dtype)


def _inproj(x2, g, w, *, tm, tn):
    n, d = x2.shape
    d_in = w.shape[1]
    return pl.pallas_call(
        _inproj_kernel,
        grid=(n // tm, d_in // tn),
        in_specs=[
            pl.BlockSpec((tm, d), lambda i, j: (i, 0)),
            pl.BlockSpec((1, d), lambda i, j: (0, 0)),
            pl.BlockSpec((d, tn), lambda i, j: (0, j)),
        ],
        out_specs=pl.BlockSpec((tm, tn), lambda i, j: (i, j)),
        out_shape=jax.ShapeDtypeStruct((n, d_in), BF16),
        scratch_shapes=[pltpu.VMEM((tm, d), BF16)],
        compiler_params=pltpu.CompilerParams(
            dimension_semantics=("parallel", "arbitrary"),
            vmem_limit_bytes=V7X_VMEM_LIMIT_BYTES),
    )(x2, g, w)


def _inproj_t(x2, g, w, wt, *, tm, col_group, n_scaled, scale):
    n, d = x2.shape
    d_in = w.shape[1]
    d_t = wt.shape[0]
    const = lambda shape: pl.BlockSpec(shape, lambda i: (0, 0), pipeline_mode=pl.Buffered(1))
    return pl.pallas_call(
        functools.partial(_inproj_t_kernel, n_scaled=n_scaled, scale=scale, col_group=col_group),
        grid=(n // tm,),
        in_specs=[
            pl.BlockSpec((tm, d), lambda i: (i, 0)),
            pl.BlockSpec((1, d), lambda i: (0, 0)),
            const(w.shape), const(wt.shape),
        ],
        out_specs=[pl.BlockSpec((tm, d_in), lambda i: (i, 0)),
                   pl.BlockSpec((d_t, tm), lambda i: (0, i))],
        out_shape=[jax.ShapeDtypeStruct((n, d_in), BF16), jax.ShapeDtypeStruct((d_t, n), BF16)],
        compiler_params=pltpu.CompilerParams(
            dimension_semantics=("parallel",),
            vmem_limit_bytes=V7X_VMEM_LIMIT_BYTES),
    )(x2, g, w, wt)


ATTN_TQ = 1024
ATTN_SUB = 256
ATTN_CHUNK = 1024
ATTN_VT_COLS = 512
ATTN_PIECE = 256
N_FEAT = 4
LOG2_E = math.log2(math.e)
SKIP_BELOW = 140.0
SINGLE_PASS_GAP = 100.0


def _attn_block(i, slopes_ref, qt_ref, k_ref, vt_ref, mk_ref, mvt_ref, lq1_ref, lk1_ref, lq2_ref, lk2_ref,
                 hg_ref, o_ref, mvt_s, vt_s, vtp_s, kpos_s, kabs_s, qt_s, s_s, sm_s, m_s, acc_s, *, lam_init):
    h = pl.program_id(1)
    slope = slopes_ref[h]
    inv_slope = slopes_ref[DA_HEADS + h]
    tq, sub, chunk, vtc, piece = ATTN_TQ, ATTN_SUB, ATTN_CHUNK, ATTN_VT_COLS, ATTN_PIECE
    q0 = i * tq
    n_sub = tq // sub
    dv = vt_ref.shape[0]
    n_cols = 2 * tq
    neg_inf = jnp.float32(-jnp.inf)
    lane_row = lax.broadcasted_iota(jnp.int32, (1, LANES), 1)

    def prepare():
        n_ones = vt_s.shape[1] - dv
        for c in range(vt_s.shape[0]):
            vt_s[c, :dv, :] = vt_ref[:, c * vtc:(c + 1) * vtc]
            vt_s[c, dv:, :] = jnp.ones((n_ones, vtc), vt_s.dtype)
        for c in range(vtp_s.shape[0]):
            vtp_s[c, :dv, :] = vt_ref[:, c * piece:(c + 1) * piece]
            vtp_s[c, dv:, :] = jnp.ones((n_ones, piece), vtp_s.dtype)
        mvt_s[:dv, :] = mvt_ref[...]
        mvt_s[dv:, :] = jnp.ones((n_ones, LANES), mvt_s.dtype)
        pos = lax.broadcasted_iota(jnp.int32, kpos_s.shape, 0).astype(F32) * slope
        plane = lax.broadcasted_iota(jnp.int32, kpos_s.shape, 1)
        pos_hi = pos.astype(BF16).astype(F32)
        kpos_s[...] = jnp.where(plane == 0, pos_hi, jnp.where(plane == 1, pos - pos_hi, 0.0))
        frow = lax.broadcasted_iota(jnp.int32, (LANES, n_cols), 0)
        qt_s[:LANES, :] = jnp.zeros((LANES, n_cols), qt_s.dtype)
        qt_s[LANES:, :] = jnp.where(frow < N_FEAT, 1.0, 0.0).astype(qt_s.dtype)
        kabs = jnp.max(jnp.abs(k_ref[...].astype(F32)), axis=0, keepdims=True)
        kabs_s[...] = jnp.broadcast_to(kabs, kabs_s.shape).T

    if i == 0:
        prepare()

    for c in range(n_sub):
        qt_s[:DA_HEAD_DIM, 2 * c * sub:(2 * c + 1) * sub] = qt_ref[:DA_HEAD_DIM, q0 + c * sub:q0 + (c + 1) * sub]
        qt_s[DA_HEAD_DIM:LANES, (2 * c + 1) * sub:(2 * c + 2) * sub] = qt_ref[DA_HEAD_DIM:, q0 + c * sub:q0 + (c + 1) * sub]

    q_prod = jnp.abs(qt_ref[:, q0:q0 + tq].astype(F32)) * jnp.tile(kabs_s[...], (1, tq // LANES))
    b1 = jnp.sum(q_prod[:DA_HEAD_DIM], axis=0, keepdims=True)
    b2 = jnp.sum(q_prod[DA_HEAD_DIM:], axis=0, keepdims=True)
    q_bound = jnp.concatenate(
        [b[:, c * sub:(c + 1) * sub] for c in range(n_sub) for b in (b1, b2)], axis=1)

    def key_feats(row0, n_rows, offset):
        off = jnp.full((1, LANES), offset, F32)
        off_hi = off.astype(BF16).astype(F32)
        off_row = jnp.where(lane_row == 2, off_hi, jnp.where(lane_row == 3, off - off_hi, 0.0))
        return (kpos_s[row0:row0 + n_rows, :] + off_row).astype(BF16)

    def with_feats(k_blk, row0, offset):
        return jnp.concatenate([k_blk, key_feats(row0, k_blk.shape[0], offset)], axis=1)

    diag = q0
    kr = lax.broadcasted_iota(jnp.int32, (sub, 2 * sub), 0)
    qc = lax.broadcasted_iota(jnp.int32, (sub, 2 * sub), 1)
    causal = kr <= (qc & (sub - 1))
    for r in range(n_sub):
        c0 = 2 * sub * r
        lhs = with_feats(k_ref[pl.ds(diag + r * sub, sub), :], r * sub, 0.0)
        if r == 0:
            lhs = jnp.concatenate(
                [lhs, with_feats(mk_ref[...], 0, -float(q0 + N_META) * slope)], axis=0)
        s = _dot(lhs, qt_s[:, c0:])
        s_first = jnp.where(causal, s[:sub, :2 * sub], neg_inf)
        s_top = s_first if r == n_sub - 1 else jnp.concatenate([s_first, s[:sub, 2 * sub:]], axis=1)
        s_s[pl.ds(diag + r * sub, sub), c0:] = s_top
        m_r = jnp.max(s_top, axis=0, keepdims=True)
        if r == 0:
            sm_s[...] = s[sub:]
            m_s[...] = jnp.maximum(m_r, jnp.max(s[sub:], axis=0, keepdims=True))
        else:
            m_s[:, c0:] = jnp.maximum(m_s[:, c0:], m_r)

    gap = jnp.max(q_bound - m_s[...])
    dist = (SKIP_BELOW + gap) * inv_slope
    n_chunks = i * (tq // chunk)
    first_key = jnp.clip(jnp.floor(float(q0) - dist + 1.0).astype(jnp.int32), 0, q0)
    first_chunk = first_key // chunk
    first_piece = (first_key - first_chunk * chunk) // piece
    in_pieces = jnp.logical_and(first_chunk < n_chunks, first_piece >= chunk // piece // 2)
    piece_lo = jnp.where(in_pieces, first_piece, chunk // piece)
    chunk_lo = first_chunk + in_pieces.astype(jnp.int32)

    def chunk_start(t):
        return pl.multiple_of(t * chunk, chunk)

    def piece_start(p):
        return pl.multiple_of(first_chunk * chunk + p * piece, piece)

    def score_rows(start, n_rows):
        s = _dot(with_feats(k_ref[pl.ds(start, n_rows), :], 0, (start - q0).astype(F32) * slope), qt_s[...])
        s_s[pl.ds(start, n_rows), :] = s
        return s

    def accumulate_rows(start, n_rows, x):
        p = jnp.exp2(x).astype(BF16)
        if n_rows == piece:
            upd = _dot(vtp_s[start // piece], p)
        else:
            upd = _dot(vt_s[start // vtc], p[:vtc])
            for c in range(1, n_rows // vtc):
                upd = upd + _dot(vt_s[start // vtc + c], p[c * vtc:(c + 1) * vtc])
        acc_s[...] += upd

    def over_rows(step):
        lax.fori_loop(piece_lo, chunk // piece, lambda p, c: step(piece_start(p), piece, c), 0)
        lax.fori_loop(chunk_lo, n_chunks, lambda t, c: step(chunk_start(t), chunk, c), 0)

    if i > 0:
        acc_s[...] = jnp.zeros(acc_s.shape, acc_s.dtype)
        single_pass = gap < SINGLE_PASS_GAP

        @pl.when(single_pass)
        def _():
            m0 = m_s[...]

            def step(start, n_rows, carry):
                accumulate_rows(start, n_rows, score_rows(start, n_rows) - m0)
                return carry

            over_rows(step)

        @pl.when(jnp.logical_not(single_pass))
        def _():
            def step1(start, n_rows, carry):
                s = score_rows(start, n_rows)
                m_s[...] = jnp.maximum(m_s[...], jnp.max(s, axis=0, keepdims=True))
                return carry

            over_rows(step1)
            m1 = m_s[...]

            def step2(start, n_rows, carry):
                accumulate_rows(start, n_rows, (s_s[pl.ds(start, n_rows), :] - m1).astype(BF16))
                return carry

            over_rows(step2)

    m = m_s[...]
    for r in range(n_sub):
        c0 = 2 * sub * r
        vt_r = vt_s[(diag + r * sub) // vtc][:, (r * sub) % vtc:(r * sub) % vtc + sub]
        p = jnp.exp2((s_s[pl.ds(diag + r * sub, sub), c0:] - m[:, c0:]).astype(BF16))
        if r == 0:
            p = jnp.concatenate([p, jnp.exp2((sm_s[...] - m).astype(BF16)),
                                 jnp.zeros((LANES - N_META, n_cols), BF16)], axis=0)
            upd = _dot(jnp.concatenate([vt_r, mvt_s[...]], axis=1), p)
            acc_s[...] = upd if i == 0 else acc_s[...] + upd
        else:
            acc_s[:, c0:] += _dot(vt_r, p)

    lam = (jnp.exp(jnp.sum(lq1_ref[...] * lk1_ref[...], axis=1, keepdims=True))
           - jnp.exp(jnp.sum(lq2_ref[...] * lk2_ref[...], axis=1, keepdims=True)) + lam_init)
    inv_l = 1.0 / acc_s[dv:dv + 1, :]
    for c in range(n_sub):
        c0 = 2 * sub * c
        o1 = acc_s[:dv, c0:c0 + sub] * inv_l[:, c0:c0 + sub]
        o2 = acc_s[:dv, c0 + sub:c0 + 2 * sub] * inv_l[:, c0 + sub:c0 + 2 * sub]
        ot = o1 - lam * o2
        yt = ot * lax.rsqrt(jnp.mean(ot * ot, axis=0, keepdims=True) + NORM_EPS)
        o_ref[q0 + c * sub:q0 + (c + 1) * sub, :] = (yt.T * hg_ref[...] * (1.0 - lam_init)).astype(o_ref.dtype)


def _attn_kernel(*refs, lam_init):
    qt_ref = refs[1]
    for i in range(qt_ref.shape[1] // ATTN_TQ):
        _attn_block(i, *refs, lam_init=lam_init)


def _diff_attention(proj, proj_t, mk, mvt, lq1, lk1, lq2, lk2, head_g, *, batch, seq, lam_init):
    n = batch * seq
    tq = ATTN_TQ
    assert seq % tq == 0 and tq % ATTN_CHUNK == 0 and ATTN_CHUNK % ATTN_VT_COLS == 0
    dv = DA_VDIM
    slopes = 2.0 ** (-8.0 * (np.arange(DA_HEADS, dtype=np.float64) + 1.0) / DA_HEADS) * LOG2_E
    slopes = jnp.asarray(np.concatenate([slopes, 1.0 / slopes]), F32)
    vec = lambda: pl.BlockSpec((1, DA_HEAD_DIM), lambda b, h: (0, 0))
    n_cols = 2 * tq
    n_vrows = dv + BF16_SUBLANES
    return pl.pallas_call(
        functools.partial(_attn_kernel, lam_init=lam_init),
        grid=(batch, DA_HEADS),
        in_specs=[
            pl.BlockSpec(memory_space=pltpu.SMEM),
            pl.BlockSpec((dv, seq), lambda b, h: (h, b)),
            pl.BlockSpec((seq, dv), lambda b, h: (b, PROJ_COLS["k_a"] * DA_HEADS + h)),
            pl.BlockSpec((dv, seq), lambda b, h: (DA_HEADS + h, b)),
            pl.BlockSpec((N_META, dv), lambda b, h: (0, h)),
            pl.BlockSpec((dv, LANES), lambda b, h: (h, 0)),
            vec(), vec(), vec(), vec(),
            pl.BlockSpec((1, dv), lambda b, h: (0, 0)),
        ],
        out_specs=pl.BlockSpec((seq, dv), lambda b, h: (b, h)),
        out_shape=jax.ShapeDtypeStruct((n, DA_HEADS * dv), BF16),
        scratch_shapes=[pltpu.VMEM((n_vrows, LANES), BF16),
                        pltpu.VMEM((seq // ATTN_VT_COLS, n_vrows, ATTN_VT_COLS), BF16),
                        pltpu.VMEM((seq // ATTN_PIECE, n_vrows, ATTN_PIECE), BF16),
                        pltpu.VMEM((max(tq, ATTN_CHUNK), LANES), F32),
                        pltpu.VMEM((LANES, LANES), F32),
                        pltpu.VMEM((2 * LANES, n_cols), BF16),
                        pltpu.VMEM((seq, n_cols), F32),
                        pltpu.VMEM((N_META, n_cols), F32),
                        pltpu.VMEM((1, n_cols), F32),
                        pltpu.VMEM((n_vrows, n_cols), F32)],
        compiler_params=pltpu.CompilerParams(
            dimension_semantics=("parallel", "parallel"),
            vmem_limit_bytes=V7X_VMEM_LIMIT_BYTES),
    )(slopes, proj_t, proj, proj_t, mk, mvt, lq1, lk1, lq2, lk2, head_g)


def _ret_merge_kernel(lg_ref, x_ref, yda_ref, q_ref, k_ref, v0_ref, v1_ref, gate0_ref, gate1_ref, ga_ref, gr_ref,
                      mk_ref, mv0_ref, mv1_ref, wda_ref, wret_ref, wout_ref, o_ref, state_s, dec_s,
                      *, chunk, k_scale):
    ci = pl.program_id(1)
    n_heads = state_s.shape[0]
    dk, dv = state_s.shape[1], state_s.shape[2]
    rows = lax.broadcasted_iota(jnp.int32, (chunk, 1), 0).astype(F32)
    ks = jnp.asarray(k_scale, BF16)
    v_refs, mv_refs, gate_refs = (v0_ref, v1_ref), (mv0_ref, mv1_ref), (gate0_ref, gate1_ref)
    heads_per_ref = v0_ref.shape[1] // dv

    def head_cols(refs, hh):
        return refs[hh // heads_per_ref], slice((hh % heads_per_ref) * dv, (hh % heads_per_ref + 1) * dv)

    @pl.when(ci == 0)
    def _():
        ri = lax.broadcasted_iota(jnp.int32, (chunk, chunk), 0)
        cc = lax.broadcasted_iota(jnp.int32, (chunk, chunk), 1)
        diff = (ri - cc).astype(F32)
        for hh in range(n_heads):
            lg = lg_ref[hh]
            dec_s[hh] = jnp.where(diff >= 0, jnp.exp(jnp.maximum(diff, 0.0) * lg), 0.0)
            kd = ((mk_ref[:, hh * dk:(hh + 1) * dk] * ks).astype(F32) * jnp.exp((chunk - 1.0 - rows) * lg))
            mv_ref, cols = head_cols(mv_refs, hh)
            state_s[hh] = _dot_tn(kd.astype(BF16), mv_ref[:, cols])

    y_heads = []
    for hh in range(n_heads):
        lg = lg_ref[hh]
        end_dec = jnp.exp((chunk - 1.0 - rows) * lg)
        cross_dec = jnp.exp((rows + 1.0) * lg)
        chunk_dec = jnp.exp(jnp.full((1, 1), float(chunk), F32) * lg)
        v_ref, vcols = head_cols(v_refs, hh)
        gate_ref, gcols = head_cols(gate_refs, hh)
        state = state_s[hh]
        y_chunks = []
        for r0 in range(0, x_ref.shape[0], chunk):
            rs = slice(r0, r0 + chunk)
            qc = q_ref[rs, hh * dk:(hh + 1) * dk]
            kc = k_ref[rs, hh * dk:(hh + 1) * dk] * ks
            vc = v_ref[rs, vcols]
            a = _dot_nt(qc, kc) * dec_s[hh]
            o = _dot(a.astype(BF16), vc) + _dot(qc, state.astype(BF16)) * cross_dec
            kd = (kc.astype(F32) * end_dec).astype(BF16)
            state = state * chunk_dec + _dot_tn(kd, vc)
            g = gate_ref[rs, gcols].astype(F32)
            y_chunks.append((_rms(o) * (g * jax.nn.sigmoid(g))).astype(BF16))
        state_s[hh] = state
        y_heads.append(jnp.concatenate(y_chunks, axis=0))
    y_ret = jnp.concatenate(y_heads, axis=1)

    a = _dot(yda_ref[...], wda_ref[...])
    r = _dot(y_ret, wret_ref[...])
    merged = (jax.nn.sigmoid(ga_ref[...].astype(F32)) * a
              + jax.nn.sigmoid(gr_ref[...].astype(F32)) * r)
    o_ref[...] = x_ref[...] + _dot(merged.astype(BF16), wout_ref[...])


def _ret_merge(x2, y_da, proj, mk, mv, w_da, w_ret, w_out, *, batch, seq, chunk, step_rows):
    n, d = x2.shape
    nc = seq // step_rows
    dk = d // RET_HEADS
    dv = 2 * d // RET_HEADS
    log_gamma = jnp.asarray(np.log(1.0 - 2.0 ** (-5.0 - np.arange(RET_HEADS, dtype=np.float32))), F32)
    row_tile = lambda width, col: pl.BlockSpec((step_rows, width), lambda b, c: (b * nc + c, col))
    col = lambda name, off=0: row_tile(d, PROJ_COLS[name] + off)
    meta = lambda off: pl.BlockSpec((chunk, d), lambda b, c: (0, off))
    const = lambda shape: pl.BlockSpec(shape, lambda b, c: (0, 0), pipeline_mode=pl.Buffered(1))
    return pl.pallas_call(
        functools.partial(_ret_merge_kernel, chunk=chunk, k_scale=dk ** -0.5),
        grid=(batch, nc),
        in_specs=[
            pl.BlockSpec(memory_space=pltpu.SMEM),
            row_tile(d, 0),
            row_tile(d, 0),
            col("q_r"), col("k_r"), col("v_r"), col("v_r", 1), col("gate_r"), col("gate_r", 1),
            col("g_a"), col("g_r"),
            meta(0), meta(0), meta(1),
            const(w_da.shape), const(w_ret.shape), const(w_out.shape),
        ],
        out_specs=row_tile(d, 0),
        out_shape=jax.ShapeDtypeStruct((n, d), F32),
        scratch_shapes=[pltpu.VMEM((RET_HEADS, dk, dv), F32), pltpu.VMEM((RET_HEADS, chunk, chunk), F32)],
        compiler_params=pltpu.CompilerParams(
            dimension_semantics=("parallel", "arbitrary"),
            vmem_limit_bytes=V7X_VMEM_LIMIT_BYTES),
    )(log_gamma, x2, y_da, proj, proj, proj, proj, proj, proj, proj, proj, mk, mv, mv, w_da, w_ret, w_out)


def _mlp_kernel(h_ref, g1_ref, wup_ref, wdown_ref, g2_ref, o_ref, *, tf):
    hres = h_ref[...]
    u = (_rms(hres) * g1_ref[...]).astype(BF16)
    acc = hres
    for f in range(0, wup_ref.shape[1], tf):
        a = jnp.maximum(_dot(u, wup_ref[:, f:f + tf]), 0.0)
        acc = acc + _dot((a * a).astype(BF16), wdown_ref[f:f + tf, :])
    o_ref[...] = _rms(acc) * g2_ref[...]


def _mlp(h1, g1, w_up, w_down, g2, *, tm, tf):
    n, d = h1.shape
    const = lambda shape: pl.BlockSpec(shape, lambda i: (0, 0), pipeline_mode=pl.Buffered(1))
    return pl.pallas_call(
        functools.partial(_mlp_kernel, tf=tf),
        grid=(n // tm,),
        in_specs=[
            pl.BlockSpec((tm, d), lambda i: (i, 0)),
            pl.BlockSpec((1, d), lambda i: (0, 0)),
            const(w_up.shape), const(w_down.shape),
            pl.BlockSpec((1, d), lambda i: (0, 0)),
        ],
        out_specs=pl.BlockSpec((tm, d), lambda i: (i, 0)),
        out_shape=jax.ShapeDtypeStruct((n, d), F32),
        compiler_params=pltpu.CompilerParams(
            dimension_semantics=("parallel",),
            vmem_limit_bytes=V7X_VMEM_LIMIT_BYTES),
    )(h1, g1, w_up, w_down, g2)


def _tile(n, target):
    t = min(n, target)
    assert n % t == 0, (n, t)
    return t


def kernel(x, meta_tokens, norm_mix_g, w_in, da_lambda_q1, da_lambda_k1, da_lambda_q2, da_lambda_k2,
           da_head_g, w_proj_da, w_proj_ret, w_out, norm_mlp_g, w_up, w_down, final_norm_g):
    batch, seq, d = x.shape
    assert w_in.shape[0] == 1, "single-layer block"
    assert d // RET_HEADS == 2 * LANES and DA_HEADS * DA_VDIM == d
    n = batch * seq
    layer = 0
    lam_init = 0.8 - 0.6 * math.exp(-0.3 * layer)

    x2 = x.reshape(n, d)
    w_in_b = w_in[layer].astype(BF16)
    g_mix = norm_mix_g[layer].reshape(1, d)

    w_rows = jnp.concatenate([w_in_b[:, d:2 * d], w_in_b[:, 3 * d:]], axis=1)
    w_feat = jnp.concatenate([w_in_b[:, :d], w_in_b[:, 2 * d:3 * d]], axis=1).T
    proj, proj_t = _inproj_t(x2, g_mix, w_rows, w_feat, tm=_tile(n, 512), col_group=_tile(w_rows.shape[1], 2304),
                             n_scaled=d, scale=DA_HEAD_DIM ** -0.5 * LOG2_E)
    proj_m = _inproj(meta_tokens.astype(x.dtype), g_mix, w_in_b, tm=N_META, tn=_tile(w_in_b.shape[1], 1024))

    mvt_a = jnp.pad(proj_m[:, 2 * d:3 * d].T, ((0, 0), (0, LANES - N_META)))
    row = lambda v: v[layer].reshape(1, -1).astype(F32)
    y_da = _diff_attention(proj, proj_t, proj_m[:, d:2 * d], mvt_a, row(da_lambda_q1), row(da_lambda_k1),
                           row(da_lambda_q2), row(da_lambda_k2), row(da_head_g), batch=batch, seq=seq,
                           lam_init=lam_init)

    chunk = _tile(seq, 256)
    pad_r = ((chunk - N_META, 0), (0, 0))
    mk_r = jnp.pad(proj_m[:, 4 * d:5 * d], pad_r)
    mv_r = jnp.pad(proj_m[:, 5 * d:7 * d], pad_r)
    h1 = _ret_merge(x2, y_da, proj, mk_r, mv_r, w_proj_da[layer].astype(BF16), w_proj_ret[layer].astype(BF16),
                    w_out[layer].astype(BF16), batch=batch, seq=seq, chunk=chunk, step_rows=_tile(seq, 512))
    out = _mlp(h1, norm_mlp_g[layer].reshape(1, d), w_up[layer].astype(BF16), w_down[layer].astype(BF16),
               final_norm_g.reshape(1, d), tm=_tile(n, 1024), tf=1024)
    return out.reshape(batch, seq, d)
```

```python
import functools
import math

import numpy as np
import jax
import jax.numpy as jnp
from jax import lax
from jax.experimental import pallas as pl
from jax.experimental.pallas import tpu as pltpu

N_META = 16
DA_HEADS = 8
DA_HEAD_DIM = 64
DA_VDIM = 2 * DA_HEAD_DIM
RET_HEADS = 4
NORM_EPS = 1e-6

F32 = jnp.float32
BF16 = jnp.bfloat16

V7X_VMEM_LIMIT_BYTES = 56 * 1024 * 1024
LANES = 128
BF16_SUBLANES = 16


def _dot(a, b):
    return jnp.dot(a, b, preferred_element_type=F32)


def _dot_nt(a, b):
    return lax.dot_general(a, b, (((1,), (1,)), ((), ())), preferred_element_type=F32)


def _dot_tn(a, b):
    return lax.dot_general(a, b, (((0,), (0,)), ((), ())), preferred_element_type=F32)


def _rms(xf):
    return xf * lax.rsqrt(jnp.mean(xf * xf, axis=-1, keepdims=True) + NORM_EPS)


PROJ_COLS = {"k_a": 0, "q_r": 1, "k_r": 2, "v_r": 3, "gate_r": 5, "g_a": 7, "g_r": 8}


def _inproj_kernel(x_ref, g_ref, w_ref, o_ref, u_ref):
    @pl.when(pl.program_id(1) == 0)
    def _():
        u_ref[...] = (_rms(x_ref[...]) * g_ref[...]).astype(BF16)

    o_ref[...] = _dot(u_ref[...], w_ref[...]).astype(o_ref.dtype)


def _inproj_t_kernel(x_ref, g_ref, w_ref, wt_ref, o_ref, ot_ref, *, n_scaled, scale, col_group):
    u = (_rms(x_ref[...]) * g_ref[...]).astype(BF16)
    ot = _dot_nt(wt_ref[...], u)
    ot_ref[:n_scaled, :] = (ot[:n_scaled] * scale).astype(ot_ref.dtype)
    ot_ref[n_scaled:, :] = ot[n_scaled:].astype(ot_ref.dtype)
    for c0 in range(0, w_ref.shape[1], col_group):
        o_ref[:, c0:c0 + col_group] = _dot(u, w_ref[:, c0:c0 + col_group]).astype(o_ref.dtype)


def _inproj(x2, g, w, *, tm, tn):
    n, d = x2.shape
    d_in = w.shape[1]
    return pl.pallas_call(
        _inproj_kernel,
        grid=(n // tm, d_in // tn),
        in_specs=[
            pl.BlockSpec((tm, d), lambda i, j: (i, 0)),
            pl.BlockSpec((1, d), lambda i, j: (0, 0)),
            pl.BlockSpec((d, tn), lambda i, j: (0, j)),
        ],
        out_specs=pl.BlockSpec((tm, tn), lambda i, j: (i, j)),
        out_shape=jax.ShapeDtypeStruct((n, d_in), BF16),
        scratch_shapes=[pltpu.VMEM((tm, d), BF16)],
        compiler_params=pltpu.CompilerParams(
            dimension_semantics=("parallel", "arbitrary"),
            vmem_limit_bytes=V7X_VMEM_LIMIT_BYTES),
    )(x2, g, w)


def _inproj_t(x2, g, w, wt, *, tm, col_group, n_scaled, scale):
    n, d = x2.shape
    d_in = w.shape[1]
    d_t = wt.shape[0]
    const = lambda shape: pl.BlockSpec(shape, lambda i: (0, 0), pipeline_mode=pl.Buffered(1))
    return pl.pallas_call(
        functools.partial(_inproj_t_kernel, n_scaled=n_scaled, scale=scale, col_group=col_group),
        grid=(n // tm,),
        in_specs=[
            pl.BlockSpec((tm, d), lambda i: (i, 0)),
            pl.BlockSpec((1, d), lambda i: (0, 0)),
            const(w.shape), const(wt.shape),
        ],
        out_specs=[pl.BlockSpec((tm, d_in), lambda i: (i, 0)),
                   pl.BlockSpec((d_t, tm), lambda i: (0, i))],
        out_shape=[jax.ShapeDtypeStruct((n, d_in), BF16), jax.ShapeDtypeStruct((d_t, n), BF16)],
        compiler_params=pltpu.CompilerParams(
            dimension_semantics=("parallel",),
            vmem_limit_bytes=V7X_VMEM_LIMIT_BYTES),
    )(x2, g, w, wt)


ATTN_TQ = 1024
ATTN_SUB = 256
ATTN_CHUNK = 1024
ATTN_VT_COLS = 512
ATTN_PIECE = 256
N_FEAT = 4
LOG2_E = math.log2(math.e)
SKIP_BELOW = 140.0
SINGLE_PASS_GAP = 64.0


def _attn_block(i, slopes_ref, qt_ref, k_ref, vt_ref, mk_ref, mvt_ref, lq1_ref, lk1_ref, lq2_ref, lk2_ref,
                 hg_ref, o_ref, mvt_s, vt_s, vtp_s, kpos_s, kabs_s, qt_s, s_s, sm_s, m_s, acc_s, *, lam_init):
    h = pl.program_id(1)
    slope = slopes_ref[h]
    inv_slope = slopes_ref[DA_HEADS + h]
    tq, sub, chunk, vtc, piece = ATTN_TQ, ATTN_SUB, ATTN_CHUNK, ATTN_VT_COLS, ATTN_PIECE
    q0 = i * tq
    n_sub = tq // sub
    dv = vt_ref.shape[0]
    n_cols = 2 * tq
    neg_inf = jnp.float32(-jnp.inf)
    lane_row = lax.broadcasted_iota(jnp.int32, (1, LANES), 1)

    def prepare():
        n_ones = vt_s.shape[1] - dv
        for c in range(vt_s.shape[0]):
            vt_s[c, :dv, :] = vt_ref[:, c * vtc:(c + 1) * vtc]
            vt_s[c, dv:, :] = jnp.ones((n_ones, vtc), vt_s.dtype)
        for c in range(vtp_s.shape[0]):
            vtp_s[c, :dv, :] = vt_ref[:, c * piece:(c + 1) * piece]
            vtp_s[c, dv:, :] = jnp.ones((n_ones, piece), vtp_s.dtype)
        mvt_s[:dv, :] = mvt_ref[...]
        mvt_s[dv:, :] = jnp.ones((n_ones, LANES), mvt_s.dtype)
        pos = lax.broadcasted_iota(jnp.int32, kpos_s.shape, 0).astype(F32) * slope
        plane = lax.broadcasted_iota(jnp.int32, kpos_s.shape, 1)
        pos_hi = pos.astype(BF16).astype(F32)
        kpos_s[...] = jnp.where(plane == 0, pos_hi, jnp.where(plane == 1, pos - pos_hi, 0.0))
        frow = lax.broadcasted_iota(jnp.int32, (LANES, n_cols), 0)
        qt_s[:LANES, :] = jnp.zeros((LANES, n_cols), qt_s.dtype)
        qt_s[LANES:, :] = jnp.where(frow < N_FEAT, 1.0, 0.0).astype(qt_s.dtype)
        kabs = jnp.max(jnp.abs(k_ref[...].astype(F32)), axis=0, keepdims=True)
        kabs_s[...] = jnp.broadcast_to(kabs, kabs_s.shape).T

    if i == 0:
        prepare()

    for c in range(n_sub):
        qt_s[:DA_HEAD_DIM, 2 * c * sub:(2 * c + 1) * sub] = qt_ref[:DA_HEAD_DIM, q0 + c * sub:q0 + (c + 1) * sub]
        qt_s[DA_HEAD_DIM:LANES, (2 * c + 1) * sub:(2 * c + 2) * sub] = qt_ref[DA_HEAD_DIM:, q0 + c * sub:q0 + (c + 1) * sub]

    q_prod = jnp.abs(qt_ref[:, q0:q0 + tq].astype(F32)) * jnp.tile(kabs_s[...], (1, tq // LANES))
    b1 = jnp.sum(q_prod[:DA_HEAD_DIM], axis=0, keepdims=True)
    b2 = jnp.sum(q_prod[DA_HEAD_DIM:], axis=0, keepdims=True)
    q_bound = jnp.concatenate(
        [b[:, c * sub:(c + 1) * sub] for c in range(n_sub) for b in (b1, b2)], axis=1)

    def key_feats(row0, n_rows, offset):
        off = jnp.full((1, LANES), offset, F32)
        off_hi = off.astype(BF16).astype(F32)
        off_row = jnp.where(lane_row == 2, off_hi, jnp.where(lane_row == 3, off - off_hi, 0.0))
        return (kpos_s[row0:row0 + n_rows, :] + off_row).astype(BF16)

    def with_feats(k_blk, row0, offset):
        return jnp.concatenate([k_blk, key_feats(row0, k_blk.shape[0], offset)], axis=1)

    diag = q0
    kr = lax.broadcasted_iota(jnp.int32, (sub, 2 * sub), 0)
    qc = lax.broadcasted_iota(jnp.int32, (sub, 2 * sub), 1)
    causal = kr <= (qc & (sub - 1))
    for r in range(n_sub):
        c0 = 2 * sub * r
        lhs = with_feats(k_ref[pl.ds(diag + r * sub, sub), :], r * sub, 0.0)
        if r == 0:
            lhs = jnp.concatenate(
                [lhs, with_feats(mk_ref[...], 0, -float(q0 + N_META) * slope)], axis=0)
        s = _dot(lhs, qt_s[:, c0:])
        s_first = jnp.where(causal, s[:sub, :2 * sub], neg_inf)
        s_top = s_first if r == n_sub - 1 else jnp.concatenate([s_first, s[:sub, 2 * sub:]], axis=1)
        s_s[pl.ds(diag + r * sub, sub), c0:] = s_top
        m_r = jnp.max(s_top, axis=0, keepdims=True)
        if r == 0:
            sm_s[...] = s[sub:]
            m_s[...] = jnp.maximum(m_r, jnp.max(s[sub:], axis=0, keepdims=True))
        else:
            m_s[:, c0:] = jnp.maximum(m_s[:, c0:], m_r)

    gap = jnp.max(q_bound - m_s[...])
    dist = (SKIP_BELOW + gap) * inv_slope
    n_chunks = i * (tq // chunk)
    first_key = jnp.clip(jnp.floor(float(q0) - dist + 1.0).astype(jnp.int32), 0, q0)
    first_chunk = first_key // chunk
    first_piece = (first_key - first_chunk * chunk) // piece
    in_pieces = jnp.logical_and(first_chunk < n_chunks, first_piece >= chunk // piece // 2)
    piece_lo = jnp.where(in_pieces, first_piece, chunk // piece)
    chunk_lo = first_chunk + in_pieces.astype(jnp.int32)

    def chunk_start(t):
        return pl.multiple_of(t * chunk, chunk)

    def piece_start(p):
        return pl.multiple_of(first_chunk * chunk + p * piece, piece)

    def score_rows(start, n_rows):
        s = _dot(with_feats(k_ref[pl.ds(start, n_rows), :], 0, (start - q0).astype(F32) * slope), qt_s[...])
        s_s[pl.ds(start, n_rows), :] = s
        return s

    def accumulate_rows(start, n_rows, x):
        p = jnp.exp2(x).astype(BF16)
        if n_rows == piece:
            upd = _dot(vtp_s[start // piece], p)
        else:
            upd = _dot(vt_s[start // vtc], p[:vtc])
            for c in range(1, n_rows // vtc):
                upd = upd + _dot(vt_s[start // vtc + c], p[c * vtc:(c + 1) * vtc])
        acc_s[...] += upd

    def over_rows(step):
        lax.fori_loop(piece_lo, chunk // piece, lambda p, c: step(piece_start(p), piece, c), 0)
        lax.fori_loop(chunk_lo, n_chunks, lambda t, c: step(chunk_start(t), chunk, c), 0)

    if i > 0:
        acc_s[...] = jnp.zeros(acc_s.shape, acc_s.dtype)
        single_pass = gap < SINGLE_PASS_GAP

        @pl.when(single_pass)
        def _():
            m0 = m_s[...]

            def step(start, n_rows, carry):
                accumulate_rows(start, n_rows, score_rows(start, n_rows) - m0)
                return carry

            over_rows(step)

        @pl.when(jnp.logical_not(single_pass))
        def _():
            def step1(start, n_rows, carry):
                s = score_rows(start, n_rows)
                m_s[...] = jnp.maximum(m_s[...], jnp.max(s, axis=0, keepdims=True))
                return carry

            over_rows(step1)
            m1 = m_s[...]

            def step2(start, n_rows, carry):
                accumulate_rows(start, n_rows, (s_s[pl.ds(start, n_rows), :] - m1).astype(BF16))
                return carry

            over_rows(step2)

    m = m_s[...]
    for r in range(n_sub):
        c0 = 2 * sub * r
        vt_r = vt_s[(diag + r * sub) // vtc][:, (r * sub) % vtc:(r * sub) % vtc + sub]
        p = jnp.exp2((s_s[pl.ds(diag + r * sub, sub), c0:] - m[:, c0:]).astype(BF16))
        if r == 0:
            p = jnp.concatenate([p, jnp.exp2((sm_s[...] - m).astype(BF16)),
                                 jnp.zeros((LANES - N_META, n_cols), BF16)], axis=0)
            upd = _dot(jnp.concatenate([vt_r, mvt_s[...]], axis=1), p)
            acc_s[...] = upd if i == 0 else acc_s[...] + upd
        else:
            acc_s[:, c0:] += _dot(vt_r, p)

    lam = (jnp.exp(jnp.sum(lq1_ref[...] * lk1_ref[...], axis=1, keepdims=True))
           - jnp.exp(jnp.sum(lq2_ref[...] * lk2_ref[...], axis=1, keepdims=True)) + lam_init)
    inv_l = 1.0 / acc_s[dv:dv + 1, :]
    for c in range(n_sub):
        c0 = 2 * sub * c
        o1 = acc_s[:dv, c0:c0 + sub] * inv_l[:, c0:c0 + sub]
        o2 = acc_s[:dv, c0 + sub:c0 + 2 * sub] * inv_l[:, c0 + sub:c0 + 2 * sub]
        ot = o1 - lam * o2
        yt = ot * lax.rsqrt(jnp.mean(ot * ot, axis=0, keepdims=True) + NORM_EPS)
        o_ref[q0 + c * sub:q0 + (c + 1) * sub, :] = (yt.T * hg_ref[...] * (1.0 - lam_init)).astype(o_ref.dtype)


def _attn_kernel(*refs, lam_init):
    qt_ref = refs[1]
    for i in range(qt_ref.shape[1] // ATTN_TQ):
        _attn_block(i, *refs, lam_init=lam_init)


def _diff_attention(proj, proj_t, mk, mvt, lq1, lk1, lq2, lk2, head_g, *, batch, seq, lam_init):
    n = batch * seq
    tq = ATTN_TQ
    assert seq % tq == 0 and tq % ATTN_CHUNK == 0 and ATTN_CHUNK % ATTN_VT_COLS == 0
    dv = DA_VDIM
    slopes = 2.0 ** (-8.0 * (np.arange(DA_HEADS, dtype=np.float64) + 1.0) / DA_HEADS) * LOG2_E
    slopes = jnp.asarray(np.concatenate([slopes, 1.0 / slopes]), F32)
    vec = lambda: pl.BlockSpec((1, DA_HEAD_DIM), lambda b, h: (0, 0))
    n_cols = 2 * tq
    n_vrows = dv + BF16_SUBLANES
    return pl.pallas_call(
        functools.partial(_attn_kernel, lam_init=lam_init),
        grid=(batch, DA_HEADS),
        in_specs=[
            pl.BlockSpec(memory_space=pltpu.SMEM),
            pl.BlockSpec((dv, seq), lambda b, h: (h, b)),
            pl.BlockSpec((seq, dv), lambda b, h: (b, PROJ_COLS["k_a"] * DA_HEADS + h)),
            pl.BlockSpec((dv, seq), lambda b, h: (DA_HEADS + h, b)),
            pl.BlockSpec((N_META, dv), lambda b, h: (0, h)),
            pl.BlockSpec((dv, LANES), lambda b, h: (h, 0)),
            vec(), vec(), vec(), vec(),
            pl.BlockSpec((1, dv), lambda b, h: (0, 0)),
        ],
        out_specs=pl.BlockSpec((seq, dv), lambda b, h: (b, h)),
        out_shape=jax.ShapeDtypeStruct((n, DA_HEADS * dv), BF16),
        scratch_shapes=[pltpu.VMEM((n_vrows, LANES), BF16),
                        pltpu.VMEM((seq // ATTN_VT_COLS, n_vrows, ATTN_VT_COLS), BF16),
                        pltpu.VMEM((seq // ATTN_PIECE, n_vrows, ATTN_PIECE), BF16),
                        pltpu.VMEM((max(tq, ATTN_CHUNK), LANES), F32),
                        pltpu.VMEM((LANES, LANES), F32),
                        pltpu.VMEM((2 * LANES, n_cols), BF16),
                        pltpu.VMEM((seq, n_cols), F32),
                        pltpu.VMEM((N_META, n_cols), F32),
                        pltpu.VMEM((1, n_cols), F32),
                        pltpu.VMEM((n_vrows, n_cols), F32)],
        compiler_params=pltpu.CompilerParams(
            dimension_semantics=("parallel", "parallel"),
            vmem_limit_bytes=V7X_VMEM_LIMIT_BYTES),
    )(slopes, proj_t, proj, proj_t, mk, mvt, lq1, lk1, lq2, lk2, head_g)


def _ret_merge_kernel(lg_ref, x_ref, yda_ref, q_ref, k_ref, v0_ref, v1_ref, gate0_ref, gate1_ref, ga_ref, gr_ref,
                      mk_ref, mv0_ref, mv1_ref, wda_ref, wret_ref, wout_ref, o_ref, state_s, dec_s,
                      *, chunk, k_scale):
    ci = pl.program_id(1)
    n_heads = state_s.shape[0]
    dk, dv = state_s.shape[1], state_s.shape[2]
    rows = lax.broadcasted_iota(jnp.int32, (chunk, 1), 0).astype(F32)
    ks = jnp.asarray(k_scale, BF16)
    v_refs, mv_refs, gate_refs = (v0_ref, v1_ref), (mv0_ref, mv1_ref), (gate0_ref, gate1_ref)
    heads_per_ref = v0_ref.shape[1] // dv

    def head_cols(refs, hh):
        return refs[hh // heads_per_ref], slice((hh % heads_per_ref) * dv, (hh % heads_per_ref + 1) * dv)

    @pl.when(ci == 0)
    def _():
        ri = lax.broadcasted_iota(jnp.int32, (chunk, chunk), 0)
        cc = lax.broadcasted_iota(jnp.int32, (chunk, chunk), 1)
        diff = (ri - cc).astype(F32)
        for hh in range(n_heads):
            lg = lg_ref[hh]
            dec_s[hh] = jnp.where(diff >= 0, jnp.exp(jnp.maximum(diff, 0.0) * lg), 0.0)
            kd = ((mk_ref[:, hh * dk:(hh + 1) * dk] * ks).astype(F32) * jnp.exp((chunk - 1.0 - rows) * lg))
            mv_ref, cols = head_cols(mv_refs, hh)
            state_s[hh] = _dot_tn(kd.astype(BF16), mv_ref[:, cols])

    y_heads = []
    for hh in range(n_heads):
        lg = lg_ref[hh]
        end_dec = jnp.exp((chunk - 1.0 - rows) * lg)
        cross_dec = jnp.exp((rows + 1.0) * lg)
        chunk_dec = jnp.exp(jnp.full((1, 1), float(chunk), F32) * lg)
        v_ref, vcols = head_cols(v_refs, hh)
        gate_ref, gcols = head_cols(gate_refs, hh)
        state = state_s[hh]
        y_chunks = []
        for r0 in range(0, x_ref.shape[0], chunk):
            rs = slice(r0, r0 + chunk)
            qc = q_ref[rs, hh * dk:(hh + 1) * dk]
            kc = k_ref[rs, hh * dk:(hh + 1) * dk] * ks
            vc = v_ref[rs, vcols]
            a = _dot_nt(qc, kc) * dec_s[hh]
            o = _dot(a.astype(BF16), vc) + _dot(qc, state.astype(BF16)) * cross_dec
            kd = (kc.astype(F32) * end_dec).astype(BF16)
            state = state * chunk_dec + _dot_tn(kd, vc)
            g = gate_ref[rs, gcols].astype(F32)
            y_chunks.append((_rms(o) * (g * jax.nn.sigmoid(g))).astype(BF16))
        state_s[hh] = state
        y_heads.append(jnp.concatenate(y_chunks, axis=0))
    y_ret = jnp.concatenate(y_heads, axis=1)

    a = _dot(yda_ref[...], wda_ref[...])
    r = _dot(y_ret, wret_ref[...])
    merged = (jax.nn.sigmoid(ga_ref[...].astype(F32)) * a
              + jax.nn.sigmoid(gr_ref[...].astype(F32)) * r)
    o_ref[...] = x_ref[...] + _dot(merged.astype(BF16), wout_ref[...])


def _ret_merge(x2, y_da, proj, mk, mv, w_da, w_ret, w_out, *, batch, seq, chunk, step_rows):
    n, d = x2.shape
    nc = seq // step_rows
    dk = d // RET_HEADS
    dv = 2 * d // RET_HEADS
    log_gamma = jnp.asarray(np.log(1.0 - 2.0 ** (-5.0 - np.arange(RET_HEADS, dtype=np.float32))), F32)
    row_tile = lambda width, col: pl.BlockSpec((step_rows, width), lambda b, c: (b * nc + c, col))
    col = lambda name, off=0: row_tile(d, PROJ_COLS[name] + off)
    meta = lambda off: pl.BlockSpec((chunk, d), lambda b, c: (0, off))
    const = lambda shape: pl.BlockSpec(shape, lambda b, c: (0, 0), pipeline_mode=pl.Buffered(1))
    return pl.pallas_call(
        functools.partial(_ret_merge_kernel, chunk=chunk, k_scale=dk ** -0.5),
        grid=(batch, nc),
        in_specs=[
            pl.BlockSpec(memory_space=pltpu.SMEM),
            row_tile(d, 0),
            row_tile(d, 0),
            col("q_r"), col("k_r"), col("v_r"), col("v_r", 1), col("gate_r"), col("gate_r", 1),
            col("g_a"), col("g_r"),
            meta(0), meta(0), meta(1),
            const(w_da.shape), const(w_ret.shape), const(w_out.shape),
        ],
        out_specs=row_tile(d, 0),
        out_shape=jax.ShapeDtypeStruct((n, d), F32),
        scratch_shapes=[pltpu.VMEM((RET_HEADS, dk, dv), F32), pltpu.VMEM((RET_HEADS, chunk, chunk), F32)],
        compiler_params=pltpu.CompilerParams(
            dimension_semantics=("parallel", "arbitrary"),
            vmem_limit_bytes=V7X_VMEM_LIMIT_BYTES),
    )(log_gamma, x2, y_da, proj, proj, proj, proj, proj, proj, proj, proj, mk, mv, mv, w_da, w_ret, w_out)


def _mlp_kernel(h_ref, g1_ref, wup_ref, wdown_ref, g2_ref, o_ref, *, tf):
    hres = h_ref[...]
    u = (_rms(hres) * g1_ref[...]).astype(BF16)
    acc = hres
    for f in range(0, wup_ref.shape[1], tf):
        a = jnp.maximum(_dot(u, wup_ref[:, f:f + tf]), 0.0)
        acc = acc + _dot((a * a).astype(BF16), wdown_ref[f:f + tf, :])
    o_ref[...] = _rms(acc) * g2_ref[...]


def _mlp(h1, g1, w_up, w_down, g2, *, tm, tf):
    n, d = h1.shape
    const = lambda shape: pl.BlockSpec(shape, lambda i: (0, 0), pipeline_mode=pl.Buffered(1))
    return pl.pallas_call(
        functools.partial(_mlp_kernel, tf=tf),
        grid=(n // tm,),
        in_specs=[
            pl.BlockSpec((tm, d), lambda i: (i, 0)),
            pl.BlockSpec((1, d), lambda i: (0, 0)),
            const(w_up.shape), const(w_down.shape),
            pl.BlockSpec((1, d), lambda i: (0, 0)),
        ],
        out_specs=pl.BlockSpec((tm, d), lambda i: (i, 0)),
        out_shape=jax.ShapeDtypeStruct((n, d), F32),
        compiler_params=pltpu.CompilerParams(
            dimension_semantics=("parallel",),
            vmem_limit_bytes=V7X_VMEM_LIMIT_BYTES),
    )(h1, g1, w_up, w_down, g2)


def _tile(n, target):
    t = min(n, target)
    assert n % t == 0, (n, t)
    return t


def kernel(x, meta_tokens, norm_mix_g, w_in, da_lambda_q1, da_lambda_k1, da_lambda_q2, da_lambda_k2,
           da_head_g, w_proj_da, w_proj_ret, w_out, norm_mlp_g, w_up, w_down, final_norm_g):
    batch, seq, d = x.shape
    assert w_in.shape[0] == 1, "single-layer block"
    assert d // RET_HEADS == 2 * LANES and DA_HEADS * DA_VDIM == d
    n = batch * seq
    layer = 0
    lam_init = 0.8 - 0.6 * math.exp(-0.3 * layer)

    x2 = x.reshape(n, d)
    w_in_b = w_in[layer].astype(BF16)
    g_mix = norm_mix_g[layer].reshape(1, d)

    w_rows = jnp.concatenate([w_in_b[:, d:2 * d], w_in_b[:, 3 * d:]], axis=1)
    w_feat = jnp.concatenate([w_in_b[:, :d], w_in_b[:, 2 * d:3 * d]], axis=1).T
    proj, proj_t = _inproj_t(x2, g_mix, w_rows, w_feat, tm=_tile(n, 512), col_group=_tile(w_rows.shape[1], 2304),
                             n_scaled=d, scale=DA_HEAD_DIM ** -0.5 * LOG2_E)
    proj_m = _inproj(meta_tokens.astype(x.dtype), g_mix, w_in_b, tm=N_META, tn=_tile(w_in_b.shape[1], 1024))

    mvt_a = jnp.pad(proj_m[:, 2 * d:3 * d].T, ((0, 0), (0, LANES - N_META)))
    row = lambda v: v[layer].reshape(1, -1).astype(F32)
    y_da = _diff_attention(proj, proj_t, proj_m[:, d:2 * d], mvt_a, row(da_lambda_q1), row(da_lambda_k1),
                           row(da_lambda_q2), row(da_lambda_k2), row(da_head_g), batch=batch, seq=seq,
                           lam_init=lam_init)

    chunk = _tile(seq, 256)
    pad_r = ((chunk - N_META, 0), (0, 0))
    mk_r = jnp.pad(proj_m[:, 4 * d:5 * d], pad_r)
    mv_r = jnp.pad(proj_m[:, 5 * d:7 * d], pad_r)
    h1 = _ret_merge(x2, y_da, proj, mk_r, mv_r, w_proj_da[layer].astype(BF16), w_proj_ret[layer].astype(BF16),
                    w_out[layer].astype(BF16), batch=batch, seq=seq, chunk=chunk, step_rows=_tile(seq, 512))
    out = _mlp(h1, norm_mlp_g[layer].reshape(1, d), w_up[layer].astype(BF16), w_down[layer].astype(BF16),
               final_norm_g.reshape(1, d), tm=_tile(n, 1024), tf=1024)
    return out.reshape(batch, seq, d)
```

```python
import functools
import math

import numpy as np
import jax
import jax.numpy as jnp
from jax import lax
from jax.experimental import pallas as pl
from jax.experimental.pallas import tpu as pltpu

N_META = 16
DA_HEADS = 8
DA_HEAD_DIM = 64
DA_VDIM = 2 * DA_HEAD_DIM
RET_HEADS = 4
NORM_EPS = 1e-6

F32 = jnp.float32
BF16 = jnp.bfloat16

V7X_VMEM_LIMIT_BYTES = 56 * 1024 * 1024
LANES = 128
BF16_SUBLANES = 16


def _dot(a, b):
    return jnp.dot(a, b, preferred_element_type=F32)


def _dot_nt(a, b):
    return lax.dot_general(a, b, (((1,), (1,)), ((), ())), preferred_element_type=F32)


def _dot_tn(a, b):
    return lax.dot_general(a, b, (((0,), (0,)), ((), ())), preferred_element_type=F32)


def _rms(xf):
    return xf * lax.rsqrt(jnp.mean(xf * xf, axis=-1, keepdims=True) + NORM_EPS)


PROJ_COLS = {"k_a": 0, "q_r": 1, "k_r": 2, "v_r": 3, "gate_r": 5, "g_a": 7, "g_r": 8}


def _inproj_kernel(x_ref, g_ref, w_ref, o_ref, u_ref):
    @pl.when(pl.program_id(1) == 0)
    def _():
        u_ref[...] = (_rms(x_ref[...]) * g_ref[...]).astype(BF16)

    o_ref[...] = _dot(u_ref[...], w_ref[...]).astype(o_ref.dtype)


def _inproj_t_kernel(x_ref, g_ref, w_ref, wt_ref, o_ref, ot_ref, *, n_scaled, scale, col_group):
    u = (_rms(x_ref[...]) * g_ref[...]).astype(BF16)
    ot = _dot_nt(wt_ref[...], u)
    ot_ref[:n_scaled, :] = (ot[:n_scaled] * scale).astype(ot_ref.dtype)
    ot_ref[n_scaled:, :] = ot[n_scaled:].astype(ot_ref.dtype)
    for c0 in range(0, w_ref.shape[1], col_group):
        o_ref[:, c0:c0 + col_group] = _dot(u, w_ref[:, c0:c0 + col_group]).astype(o_ref.dtype)


def _inproj(x2, g, w, *, tm, tn):
    n, d = x2.shape
    d_in = w.shape[1]
    return pl.pallas_call(
        _inproj_kernel,
        grid=(n // tm, d_in // tn),
        in_specs=[
            pl.BlockSpec((tm, d), lambda i, j: (i, 0)),
            pl.BlockSpec((1, d), lambda i, j: (0, 0)),
            pl.BlockSpec((d, tn), lambda i, j: (0, j)),
        ],
        out_specs=pl.BlockSpec((tm, tn), lambda i, j: (i, j)),
        out_shape=jax.ShapeDtypeStruct((n, d_in), BF16),
        scratch_shapes=[pltpu.VMEM((tm, d), BF16)],
        compiler_params=pltpu.CompilerParams(
            dimension_semantics=("parallel", "arbitrary"),
            vmem_limit_bytes=V7X_VMEM_LIMIT_BYTES),
    )(x2, g, w)


def _inproj_t(x2, g, w, wt, *, tm, col_group, n_scaled, scale):
    n, d = x2.shape
    d_in = w.shape[1]
    d_t = wt.shape[0]
    const = lambda shape: pl.BlockSpec(shape, lambda i: (0, 0), pipeline_mode=pl.Buffered(1))
    return pl.pallas_call(
        functools.partial(_inproj_t_kernel, n_scaled=n_scaled, scale=scale, col_group=col_group),
        grid=(n // tm,),
        in_specs=[
            pl.BlockSpec((tm, d), lambda i: (i, 0)),
            pl.BlockSpec((1, d), lambda i: (0, 0)),
            const(w.shape), const(wt.shape),
        ],
        out_specs=[pl.BlockSpec((tm, d_in), lambda i: (i, 0)),
                   pl.BlockSpec((d_t, tm), lambda i: (0, i))],
        out_shape=[jax.ShapeDtypeStruct((n, d_in), BF16), jax.ShapeDtypeStruct((d_t, n), BF16)],
        compiler_params=pltpu.CompilerParams(
            dimension_semantics=("parallel",),
            vmem_limit_bytes=V7X_VMEM_LIMIT_BYTES),
    )(x2, g, w, wt)


ATTN_TQ = 1024
ATTN_SUB = 256
ATTN_CHUNK = 1024
ATTN_VT_COLS = 512
ATTN_PIECE = 256
N_FEAT = 4
LOG2_E = math.log2(math.e)
SKIP_BELOW = 140.0
SINGLE_PASS_GAP = 64.0


def _attn_block(i, slopes_ref, qt_ref, k_ref, vt_ref, mk_ref, mvt_ref, lq1_ref, lk1_ref, lq2_ref, lk2_ref,
                 hg_ref, o_ref, mvt_s, vt_s, vtp_s, kpos_s, kabs_s, qt_s, s_s, sm_s, m_s, acc_s, *, lam_init):
    h = pl.program_id(1)
    slope = slopes_ref[h]
    inv_slope = slopes_ref[DA_HEADS + h]
    tq, sub, chunk, vtc, piece = ATTN_TQ, ATTN_SUB, ATTN_CHUNK, ATTN_VT_COLS, ATTN_PIECE
    q0 = i * tq
    n_sub = tq // sub
    dv = vt_ref.shape[0]
    n_cols = 2 * tq
    neg_inf = jnp.float32(-jnp.inf)
    lane_row = lax.broadcasted_iota(jnp.int32, (1, LANES), 1)

    def prepare():
        n_ones = vt_s.shape[1] - dv
        for c in range(vt_s.shape[0]):
            vt_s[c, :dv, :] = vt_ref[:, c * vtc:(c + 1) * vtc]
            vt_s[c, dv:, :] = jnp.ones((n_ones, vtc), vt_s.dtype)
        for c in range(vtp_s.shape[0]):
            vtp_s[c, :dv, :] = vt_ref[:, c * piece:(c + 1) * piece]
            vtp_s[c, dv:, :] = jnp.ones((n_ones, piece), vtp_s.dtype)
        mvt_s[:dv, :] = mvt_ref[...]
        mvt_s[dv:, :] = jnp.ones((n_ones, LANES), mvt_s.dtype)
        pos = lax.broadcasted_iota(jnp.int32, kpos_s.shape, 0).astype(F32) * slope
        plane = lax.broadcasted_iota(jnp.int32, kpos_s.shape, 1)
        pos_hi = pos.astype(BF16).astype(F32)
        kpos_s[...] = jnp.where(plane == 0, pos_hi, jnp.where(plane == 1, pos - pos_hi, 0.0))
        frow = lax.broadcasted_iota(jnp.int32, (LANES, n_cols), 0)
        qt_s[:LANES, :] = jnp.zeros((LANES, n_cols), qt_s.dtype)
        qt_s[LANES:, :] = jnp.where(frow < N_FEAT, 1.0, 0.0).astype(qt_s.dtype)
        kabs = jnp.max(jnp.abs(k_ref[...].astype(F32)), axis=0, keepdims=True)
        kabs_s[...] = jnp.broadcast_to(kabs, kabs_s.shape).T

    if i == 0:
        prepare()

    for c in range(n_sub):
        qt_s[:DA_HEAD_DIM, 2 * c * sub:(2 * c + 1) * sub] = qt_ref[:DA_HEAD_DIM, q0 + c * sub:q0 + (c + 1) * sub]
        qt_s[DA_HEAD_DIM:LANES, (2 * c + 1) * sub:(2 * c + 2) * sub] = qt_ref[DA_HEAD_DIM:, q0 + c * sub:q0 + (c + 1) * sub]

    q_prod = jnp.abs(qt_ref[:, q0:q0 + tq].astype(F32)) * jnp.tile(kabs_s[...], (1, tq // LANES))
    b1 = jnp.sum(q_prod[:DA_HEAD_DIM], axis=0, keepdims=True)
    b2 = jnp.sum(q_prod[DA_HEAD_DIM:], axis=0, keepdims=True)
    q_bound = jnp.concatenate(
        [b[:, c * sub:(c + 1) * sub] for c in range(n_sub) for b in (b1, b2)], axis=1)

    def key_feats(row0, n_rows, offset):
        off = jnp.full((1, LANES), offset, F32)
        off_hi = off.astype(BF16).astype(F32)
        off_row = jnp.where(lane_row == 2, off_hi, jnp.where(lane_row == 3, off - off_hi, 0.0))
        return (kpos_s[row0:row0 + n_rows, :] + off_row).astype(BF16)

    def with_feats(k_blk, row0, offset):
        return jnp.concatenate([k_blk, key_feats(row0, k_blk.shape[0], offset)], axis=1)

    diag = q0
    kr = lax.broadcasted_iota(jnp.int32, (sub, 2 * sub), 0)
    qc = lax.broadcasted_iota(jnp.int32, (sub, 2 * sub), 1)
    causal = kr <= (qc & (sub - 1))
    for r in range(n_sub):
        c0 = 2 * sub * r
        lhs = with_feats(k_ref[pl.ds(diag + r * sub, sub), :], r * sub, 0.0)
        if r == 0:
            lhs = jnp.concatenate(
                [lhs, with_feats(mk_ref[...], 0, -float(q0 + N_META) * slope)], axis=0)
        s = _dot(lhs, qt_s[:, c0:])
        s_first = jnp.where(causal, s[:sub, :2 * sub], neg_inf)
        s_top = s_first if r == n_sub - 1 else jnp.concatenate([s_first, s[:sub, 2 * sub:]], axis=1)
        s_s[pl.ds(diag + r * sub, sub), c0:] = s_top
        m_r = jnp.max(s_top, axis=0, keepdims=True)
        if r == 0:
            sm_s[...] = s[sub:]
            m_s[...] = jnp.maximum(m_r, jnp.max(s[sub:], axis=0, keepdims=True))
        else:
            m_s[:, c0:] = jnp.maximum(m_s[:, c0:], m_r)

    gap = jnp.max(q_bound - m_s[...])
    dist = (SKIP_BELOW + gap) * inv_slope
    n_chunks = i * (tq // chunk)
    first_key = jnp.clip(jnp.floor(float(q0) - dist + 1.0).astype(jnp.int32), 0, q0)
    first_chunk = first_key // chunk
    first_piece = (first_key - first_chunk * chunk) // piece
    in_pieces = jnp.logical_and(first_chunk < n_chunks, first_piece >= 1)
    piece_lo = jnp.where(in_pieces, first_piece, chunk // piece)
    chunk_lo = first_chunk + in_pieces.astype(jnp.int32)

    def chunk_start(t):
        return pl.multiple_of(t * chunk, chunk)

    def piece_start(p):
        return pl.multiple_of(first_chunk * chunk + p * piece, piece)

    def score_rows(start, n_rows):
        s = _dot(with_feats(k_ref[pl.ds(start, n_rows), :], 0, (start - q0).astype(F32) * slope), qt_s[...])
        s_s[pl.ds(start, n_rows), :] = s
        return s

    def accumulate_rows(start, n_rows, x):
        p = jnp.exp2(x).astype(BF16)
        if n_rows == piece:
            upd = _dot(vtp_s[start // piece], p)
        else:
            upd = _dot(vt_s[start // vtc], p[:vtc])
            for c in range(1, n_rows // vtc):
                upd = upd + _dot(vt_s[start // vtc + c], p[c * vtc:(c + 1) * vtc])
        acc_s[...] += upd

    def over_rows(step):
        lax.fori_loop(piece_lo, chunk // piece, lambda p, c: step(piece_start(p), piece, c), 0)
        lax.fori_loop(chunk_lo, n_chunks, lambda t, c: step(chunk_start(t), chunk, c), 0)

    if i > 0:
        acc_s[...] = jnp.zeros(acc_s.shape, acc_s.dtype)
        single_pass = gap < SINGLE_PASS_GAP

        @pl.when(single_pass)
        def _():
            m0 = m_s[...]

            def step(start, n_rows, carry):
                accumulate_rows(start, n_rows, score_rows(start, n_rows) - m0)
                return carry

            over_rows(step)

        @pl.when(jnp.logical_not(single_pass))
        def _():
            def step1(start, n_rows, carry):
                s = score_rows(start, n_rows)
                m_s[...] = jnp.maximum(m_s[...], jnp.max(s, axis=0, keepdims=True))
                return carry

            over_rows(step1)
            m1 = m_s[...]

            def step2(start, n_rows, carry):
                accumulate_rows(start, n_rows, (s_s[pl.ds(start, n_rows), :] - m1).astype(BF16))
                return carry

            over_rows(step2)

    m = m_s[...]
    for r in range(n_sub):
        c0 = 2 * sub * r
        vt_r = vt_s[(diag + r * sub) // vtc][:, (r * sub) % vtc:(r * sub) % vtc + sub]
        p = jnp.exp2((s_s[pl.ds(diag + r * sub, sub), c0:] - m[:, c0:]).astype(BF16))
        if r == 0:
            p = jnp.concatenate([p, jnp.exp2((sm_s[...] - m).astype(BF16)),
                                 jnp.zeros((LANES - N_META, n_cols), BF16)], axis=0)
            upd = _dot(jnp.concatenate([vt_r, mvt_s[...]], axis=1), p)
            acc_s[...] = upd if i == 0 else acc_s[...] + upd
        else:
            acc_s[:, c0:] += _dot(vt_r, p)

    lam = (jnp.exp(jnp.sum(lq1_ref[...] * lk1_ref[...], axis=1, keepdims=True))
           - jnp.exp(jnp.sum(lq2_ref[...] * lk2_ref[...], axis=1, keepdims=True)) + lam_init)
    inv_l = 1.0 / acc_s[dv:dv + 1, :]
    for c in range(n_sub):
        c0 = 2 * sub * c
        o1 = acc_s[:dv, c0:c0 + sub] * inv_l[:, c0:c0 + sub]
        o2 = acc_s[:dv, c0 + sub:c0 + 2 * sub] * inv_l[:, c0 + sub:c0 + 2 * sub]
        ot = o1 - lam * o2
        yt = ot * lax.rsqrt(jnp.mean(ot * ot, axis=0, keepdims=True) + NORM_EPS)
        o_ref[q0 + c * sub:q0 + (c + 1) * sub, :] = (yt.T * hg_ref[...] * (1.0 - lam_init)).astype(o_ref.dtype)


def _attn_kernel(*refs, lam_init):
    qt_ref = refs[1]
    for i in range(qt_ref.shape[1] // ATTN_TQ):
        _attn_block(i, *refs, lam_init=lam_init)


def _diff_attention(proj, proj_t, mk, mvt, lq1, lk1, lq2, lk2, head_g, *, batch, seq, lam_init):
    n = batch * seq
    tq = ATTN_TQ
    assert seq % tq == 0 and tq % ATTN_CHUNK == 0 and ATTN_CHUNK % ATTN_VT_COLS == 0
    dv = DA_VDIM
    slopes = 2.0 ** (-8.0 * (np.arange(DA_HEADS, dtype=np.float64) + 1.0) / DA_HEADS) * LOG2_E
    slopes = jnp.asarray(np.concatenate([slopes, 1.0 / slopes]), F32)
    vec = lambda: pl.BlockSpec((1, DA_HEAD_DIM), lambda b, h: (0, 0))
    n_cols = 2 * tq
    n_vrows = dv + BF16_SUBLANES
    return pl.pallas_call(
        functools.partial(_attn_kernel, lam_init=lam_init),
        grid=(batch, DA_HEADS),
        in_specs=[
            pl.BlockSpec(memory_space=pltpu.SMEM),
            pl.BlockSpec((dv, seq), lambda b, h: (h, b)),
            pl.BlockSpec((seq, dv), lambda b, h: (b, PROJ_COLS["k_a"] * DA_HEADS + h)),
            pl.BlockSpec((dv, seq), lambda b, h: (DA_HEADS + h, b)),
            pl.BlockSpec((N_META, dv), lambda b, h: (0, h)),
            pl.BlockSpec((dv, LANES), lambda b, h: (h, 0)),
            vec(), vec(), vec(), vec(),
            pl.BlockSpec((1, dv), lambda b, h: (0, 0)),
        ],
        out_specs=pl.BlockSpec((seq, dv), lambda b, h: (b, h)),
        out_shape=jax.ShapeDtypeStruct((n, DA_HEADS * dv), BF16),
        scratch_shapes=[pltpu.VMEM((n_vrows, LANES), BF16),
                        pltpu.VMEM((seq // ATTN_VT_COLS, n_vrows, ATTN_VT_COLS), BF16),
                        pltpu.VMEM((seq // ATTN_PIECE, n_vrows, ATTN_PIECE), BF16),
                        pltpu.VMEM((max(tq, ATTN_CHUNK), LANES), F32),
                        pltpu.VMEM((LANES, LANES), F32),
                        pltpu.VMEM((2 * LANES, n_cols), BF16),
                        pltpu.VMEM((seq, n_cols), F32),
                        pltpu.VMEM((N_META, n_cols), F32),
                        pltpu.VMEM((1, n_cols), F32),
                        pltpu.VMEM((n_vrows, n_cols), F32)],
        compiler_params=pltpu.CompilerParams(
            dimension_semantics=("parallel", "parallel"),
            vmem_limit_bytes=V7X_VMEM_LIMIT_BYTES),
    )(slopes, proj_t, proj, proj_t, mk, mvt, lq1, lk1, lq2, lk2, head_g)


def _ret_merge_kernel(lg_ref, x_ref, yda_ref, q_ref, k_ref, v0_ref, v1_ref, gate0_ref, gate1_ref, ga_ref, gr_ref,
                      mk_ref, mv0_ref, mv1_ref, wda_ref, wret_ref, wout_ref, o_ref, state_s, dec_s,
                      *, chunk, k_scale):
    ci = pl.program_id(1)
    n_heads = state_s.shape[0]
    dk, dv = state_s.shape[1], state_s.shape[2]
    rows = lax.broadcasted_iota(jnp.int32, (chunk, 1), 0).astype(F32)
    ks = jnp.asarray(k_scale, BF16)
    v_refs, mv_refs, gate_refs = (v0_ref, v1_ref), (mv0_ref, mv1_ref), (gate0_ref, gate1_ref)
    heads_per_ref = v0_ref.shape[1] // dv

    def head_cols(refs, hh):
        return refs[hh // heads_per_ref], slice((hh % heads_per_ref) * dv, (hh % heads_per_ref + 1) * dv)

    @pl.when(ci == 0)
    def _():
        ri = lax.broadcasted_iota(jnp.int32, (chunk, chunk), 0)
        cc = lax.broadcasted_iota(jnp.int32, (chunk, chunk), 1)
        diff = (ri - cc).astype(F32)
        for hh in range(n_heads):
            lg = lg_ref[hh]
            dec_s[hh] = jnp.where(diff >= 0, jnp.exp(jnp.maximum(diff, 0.0) * lg), 0.0)
            kd = ((mk_ref[:, hh * dk:(hh + 1) * dk] * ks).astype(F32) * jnp.exp((chunk - 1.0 - rows) * lg))
            mv_ref, cols = head_cols(mv_refs, hh)
            state_s[hh] = _dot_tn(kd.astype(BF16), mv_ref[:, cols])

    y_heads = []
    for hh in range(n_heads):
        lg = lg_ref[hh]
        end_dec = jnp.exp((chunk - 1.0 - rows) * lg)
        cross_dec = jnp.exp((rows + 1.0) * lg)
        chunk_dec = jnp.exp(jnp.full((1, 1), float(chunk), F32) * lg)
        v_ref, vcols = head_cols(v_refs, hh)
        gate_ref, gcols = head_cols(gate_refs, hh)
        state = state_s[hh]
        y_chunks = []
        for r0 in range(0, x_ref.shape[0], chunk):
            rs = slice(r0, r0 + chunk)
            qc = q_ref[rs, hh * dk:(hh + 1) * dk]
            kc = k_ref[rs, hh * dk:(hh + 1) * dk] * ks
            vc = v_ref[rs, vcols]
            a = _dot_nt(qc, kc) * dec_s[hh]
            o = _dot(a.astype(BF16), vc) + _dot(qc, state.astype(BF16)) * cross_dec
            kd = (kc.astype(F32) * end_dec).astype(BF16)
            state = state * chunk_dec + _dot_tn(kd, vc)
            g = gate_ref[rs, gcols].astype(F32)
            y_chunks.append((_rms(o) * (g * jax.nn.sigmoid(g))).astype(BF16))
        state_s[hh] = state
        y_heads.append(jnp.concatenate(y_chunks, axis=0))
    y_ret = jnp.concatenate(y_heads, axis=1)

    a = _dot(yda_ref[...], wda_ref[...])
    r = _dot(y_ret, wret_ref[...])
    merged = (jax.nn.sigmoid(ga_ref[...].astype(F32)) * a
              + jax.nn.sigmoid(gr_ref[...].astype(F32)) * r)
    o_ref[...] = x_ref[...] + _dot(merged.astype(BF16), wout_ref[...])


def _ret_merge(x2, y_da, proj, mk, mv, w_da, w_ret, w_out, *, batch, seq, chunk, step_rows):
    n, d = x2.shape
    nc = seq // step_rows
    dk = d // RET_HEADS
    dv = 2 * d // RET_HEADS
    log_gamma = jnp.asarray(np.log(1.0 - 2.0 ** (-5.0 - np.arange(RET_HEADS, dtype=np.float32))), F32)
    row_tile = lambda width, col: pl.BlockSpec((step_rows, width), lambda b, c: (b * nc + c, col))
    col = lambda name, off=0: row_tile(d, PROJ_COLS[name] + off)
    meta = lambda off: pl.BlockSpec((chunk, d), lambda b, c: (0, off))
    const = lambda shape: pl.BlockSpec(shape, lambda b, c: (0, 0), pipeline_mode=pl.Buffered(1))
    return pl.pallas_call(
        functools.partial(_ret_merge_kernel, chunk=chunk, k_scale=dk ** -0.5),
        grid=(batch, nc),
        in_specs=[
            pl.BlockSpec(memory_space=pltpu.SMEM),
            row_tile(d, 0),
            row_tile(d, 0),
            col("q_r"), col("k_r"), col("v_r"), col("v_r", 1), col("gate_r"), col("gate_r", 1),
            col("g_a"), col("g_r"),
            meta(0), meta(0), meta(1),
            const(w_da.shape), const(w_ret.shape), const(w_out.shape),
        ],
        out_specs=row_tile(d, 0),
        out_shape=jax.ShapeDtypeStruct((n, d), F32),
        scratch_shapes=[pltpu.VMEM((RET_HEADS, dk, dv), F32), pltpu.VMEM((RET_HEADS, chunk, chunk), F32)],
        compiler_params=pltpu.CompilerParams(
            dimension_semantics=("parallel", "arbitrary"),
            vmem_limit_bytes=V7X_VMEM_LIMIT_BYTES),
    )(log_gamma, x2, y_da, proj, proj, proj, proj, proj, proj, proj, proj, mk, mv, mv, w_da, w_ret, w_out)


def _mlp_kernel(h_ref, g1_ref, wup_ref, wdown_ref, g2_ref, o_ref, *, tf):
    hres = h_ref[...]
    u = (_rms(hres) * g1_ref[...]).astype(BF16)
    acc = hres
    for f in range(0, wup_ref.shape[1], tf):
        a = jnp.maximum(_dot(u, wup_ref[:, f:f + tf]), 0.0)
        acc = acc + _dot((a * a).astype(BF16), wdown_ref[f:f + tf, :])
    o_ref[...] = _rms(acc) * g2_ref[...]


def _mlp(h1, g1, w_up, w_down, g2, *, tm, tf):
    n, d = h1.shape
    const = lambda shape: pl.BlockSpec(shape, lambda i: (0, 0), pipeline_mode=pl.Buffered(1))
    return pl.pallas_call(
        functools.partial(_mlp_kernel, tf=tf),
        grid=(n // tm,),
        in_specs=[
            pl.BlockSpec((tm, d), lambda i: (i, 0)),
            pl.BlockSpec((1, d), lambda i: (0, 0)),
            const(w_up.shape), const(w_down.shape),
            pl.BlockSpec((1, d), lambda i: (0, 0)),
        ],
        out_specs=pl.BlockSpec((tm, d), lambda i: (i, 0)),
        out_shape=jax.ShapeDtypeStruct((n, d), F32),
        compiler_params=pltpu.CompilerParams(
            dimension_semantics=("parallel",),
            vmem_limit_bytes=V7X_VMEM_LIMIT_BYTES),
    )(h1, g1, w_up, w_down, g2)


def _tile(n, target):
    t = min(n, target)
    assert n % t == 0, (n, t)
    return t


def kernel(x, meta_tokens, norm_mix_g, w_in, da_lambda_q1, da_lambda_k1, da_lambda_q2, da_lambda_k2,
           da_head_g, w_proj_da, w_proj_ret, w_out, norm_mlp_g, w_up, w_down, final_norm_g):
    batch, seq, d = x.shape
    assert w_in.shape[0] == 1, "single-layer block"
    assert d // RET_HEADS == 2 * LANES and DA_HEADS * DA_VDIM == d
    n = batch * seq
    layer = 0
    lam_init = 0.8 - 0.6 * math.exp(-0.3 * layer)

    x2 = x.reshape(n, d)
    w_in_b = w_in[layer].astype(BF16)
    g_mix = norm_mix_g[layer].reshape(1, d)

    w_rows = jnp.concatenate([w_in_b[:, d:2 * d], w_in_b[:, 3 * d:]], axis=1)
    w_feat = jnp.concatenate([w_in_b[:, :d], w_in_b[:, 2 * d:3 * d]], axis=1).T
    proj, proj_t = _inproj_t(x2, g_mix, w_rows, w_feat, tm=_tile(n, 512), col_group=_tile(w_rows.shape[1], 2304),
                             n_scaled=d, scale=DA_HEAD_DIM ** -0.5 * LOG2_E)
    proj_m = _inproj(meta_tokens.astype(x.dtype), g_mix, w_in_b, tm=N_META, tn=_tile(w_in_b.shape[1], 1024))

    mvt_a = jnp.pad(proj_m[:, 2 * d:3 * d].T, ((0, 0), (0, LANES - N_META)))
    row = lambda v: v[layer].reshape(1, -1).astype(F32)
    y_da = _diff_attention(proj, proj_t, proj_m[:, d:2 * d], mvt_a, row(da_lambda_q1), row(da_lambda_k1),
                           row(da_lambda_q2), row(da_lambda_k2), row(da_head_g), batch=batch, seq=seq,
                           lam_init=lam_init)

    chunk = _tile(seq, 256)
    pad_r = ((chunk - N_META, 0), (0, 0))
    mk_r = jnp.pad(proj_m[:, 4 * d:5 * d], pad_r)
    mv_r = jnp.pad(proj_m[:, 5 * d:7 * d], pad_r)
    h1 = _ret_merge(x2, y_da, proj, mk_r, mv_r, w_proj_da[layer].astype(BF16), w_proj_ret[layer].astype(BF16),
                    w_out[layer].astype(BF16), batch=batch, seq=seq, chunk=chunk, step_rows=_tile(seq, 512))
    out = _mlp(h1, norm_mlp_g[layer].reshape(1, d), w_up[layer].astype(BF16), w_down[layer].astype(BF16),
               final_norm_g.reshape(1, d), tm=_tile(n, 1024), tf=1024)
    return out.reshape(batch, seq, d)
```
